```python
import jax, jax.numpy as jnp
from jax import lax
import numpy as np

D_MODEL = 4096
BATCH = 1
SEQ = 8192
DEPTH = 2

CTX_LEN = 256
GRID_W = 64
EPS = 1e-6

ATTN_HEAD_DIM = 128
ATTN_HEADS = (D_MODEL // 2) // ATTN_HEAD_DIM
ATTN_KV_HEADS = ATTN_HEADS // 4
ATTN_Q_BLOCK = 128
ROPE_THETA = 10000.0
ROPE_FREQS = ATTN_HEAD_DIM // 4

MLSTM_HEADS = 4
MLSTM_DV = (D_MODEL // 2) // MLSTM_HEADS
MLSTM_DK = MLSTM_DV // 2
MLSTM_CHUNK = 64
GATE_SOFTCAP = 15.0

POOL_WINDOWS = (2, 4, 8, 16)
POOL_GROUP = D_MODEL // len(POOL_WINDOWS)

N_EXPERTS = 64
D_EXPERT = 256
TOP_K = 8
N_GROUPS = 8
TOPK_GROUPS = 4
ROUTED_SCALE = 2.5

N_EVEN = (DEPTH + 1) // 2
N_ODD = DEPTH // 2

IN_SIZES = (ATTN_HEADS * ATTN_HEAD_DIM, ATTN_KV_HEADS * ATTN_HEAD_DIM, ATTN_KV_HEADS * ATTN_HEAD_DIM,
            MLSTM_HEADS * MLSTM_DK, MLSTM_HEADS * MLSTM_DK, MLSTM_HEADS * MLSTM_DV, MLSTM_HEADS * MLSTM_DV,
            4 * MLSTM_HEADS)
IN_SPLITS = tuple(int(s) for s in np.cumsum(IN_SIZES)[:-1])
IN_COLS = sum(IN_SIZES)

kernel_name = "hybrid_diffusion_gqa_mlstm_pool_moe"


def rmsnorm(x, g):
    xf = x.astype(jnp.float32)
    y = xf * lax.rsqrt(jnp.mean(xf * xf, axis=-1, keepdims=True) + EPS)
    return (y * g.astype(jnp.float32)).astype(x.dtype)


def modulate(x, shift, scale):
    return x * (1.0 + scale) + shift


def to_heads(t, n_heads):
    b, s, _ = t.shape
    return t.reshape(b, s, n_heads, -1).transpose(0, 2, 1, 3)


def merge_heads(t):
    b, h, s, d = t.shape
    return t.transpose(0, 2, 1, 3).reshape(b, s, h * d)


def axial_rope_tables(n_tokens):
    rows = n_tokens // GRID_W
    row = jnp.repeat(jnp.arange(rows, dtype=jnp.float32), GRID_W)
    col = jnp.tile(jnp.arange(GRID_W, dtype=jnp.float32), rows)
    inv_freq = ROPE_THETA ** (-jnp.arange(ROPE_FREQS, dtype=jnp.float32) / ROPE_FREQS)
    ang = jnp.stack([row, col], axis=-1)[..., None] * inv_freq
    return jnp.cos(ang), jnp.sin(ang)


def apply_rope(x, cos, sin):
    xs = x.reshape(x.shape[:-1] + (2, 2, ROPE_FREQS))
    x1, x2 = xs[..., 0, :], xs[..., 1, :]
    cos, sin = cos.astype(x.dtype), sin.astype(x.dtype)
    out = jnp.stack([x1 * cos - x2 * sin, x2 * cos + x1 * sin], axis=-2)
    return out.reshape(x.shape)


def block_attention(q, k, v):
    b, hq, s, dh = q.shape
    hkv = k.shape[1]
    grp = hq // hkv
    nb = s // ATTN_Q_BLOCK
    qb = q.reshape(b, hkv, grp, nb, ATTN_Q_BLOCK, dh).transpose(3, 0, 1, 2, 4, 5)
    scale = dh ** -0.5

    def one_block(qi):
        sc = jnp.einsum('bhgqd,bhkd->bhgqk', qi, k).astype(jnp.float32) * scale
        p = jax.nn.softmax(sc, axis=-1).astype(v.dtype)
        return jnp.einsum('bhgqk,bhkd->bhgqd', p, v)

    o = lax.map(one_block, qb)
    return o.transpose(1, 2, 3, 0, 4, 5).reshape(b, hq, s, dh)


def mlstm_gates(raw, bias):
    pre = raw.astype(jnp.float32) + bias.astype(jnp.float32)
    pre = GATE_SOFTCAP * jnp.tanh(pre / GATE_SOFTCAP)
    b, s, _ = pre.shape
    pre = pre.reshape(b, s, 4, MLSTM_HEADS).transpose(2, 0, 3, 1)
    return (pre[0], jax.nn.log_sigmoid(pre[1]), pre[2], jax.nn.log_sigmoid(pre[3]))


def mlstm_scan(q, k, v, log_i, log_f, state):
    b, h, s, _ = q.shape
    L = MLSTM_CHUNK
    nc = s // L

    def chunks(t):
        return jnp.moveaxis(t.reshape((b, h, nc, L) + t.shape[3:]), 2, 0)

    tri = jnp.tril(jnp.ones((L, L), dtype=bool))

    def step(carry, inp):
        C, n, m = carry
        qc, kc, vc, ic, fc = inp
        bcum = jnp.cumsum(fc, axis=-1)
        log_d = jnp.where(tri, bcum[..., :, None] - bcum[..., None, :] + ic[..., None, :], -jnp.inf)
        log_inter = bcum + m[..., None]
        m_t = jnp.maximum(log_inter, jnp.max(log_d, axis=-1))
        dmat = jnp.exp(log_d - m_t[..., None])
        w_inter = jnp.exp(log_inter - m_t)
        sc = jnp.einsum('bhtd,bhsd->bhts', qc, kc) * dmat
        num = jnp.einsum('bhts,bhsv->bhtv', sc, vc) + w_inter[..., None] * jnp.einsum('bhvd,bhtd->bhtv', C, qc)
        den = jnp.sum(sc, axis=-1) + w_inter * jnp.einsum('bhd,bhtd->bht', n, qc)
        h_out = num / jnp.maximum(jnp.abs(den), jnp.exp(-m_t))[..., None]
        g = bcum[..., -1]
        log_w = g[..., None] - bcum + ic
        m_new = jnp.maximum(g + m, jnp.max(log_w, axis=-1))
        w = jnp.exp(log_w - m_new[..., None])
        decay = jnp.exp(g + m - m_new)
        C = decay[..., None, None] * C + jnp.einsum('bhs,bhsv,bhsd->bhvd', w, vc, kc)
        n = decay[..., None] * n + jnp.einsum('bhs,bhsd->bhd', w, kc)
        return (C, n, m_new), h_out

    state, hs = lax.scan(step, state, (chunks(q), chunks(k), chunks(v), chunks(log_i), chunks(log_f)))
    return state, jnp.moveaxis(hs, 0, 2).reshape(b, h, s, -1)


def mlstm_stream(mq, mk, mv, mg, gate_bias):
    q = to_heads(mq, MLSTM_HEADS).astype(jnp.float32) * (MLSTM_DK ** -0.5)
    k = to_heads(mk, MLSTM_HEADS).astype(jnp.float32)
    v = to_heads(mv, MLSTM_HEADS).astype(jnp.float32)
    return (q, k, v) + mlstm_gates(mg, gate_bias)


def mlstm_bidirectional(sc, sl):
    b, h, _, dk = sc[0].shape
    dv = sc[2].shape[-1]
    zero = (jnp.zeros((b, h, dv, dk), jnp.float32), jnp.zeros((b, h, dk), jnp.float32),
            jnp.zeros((b, h), jnp.float32))

    def rev(t):
        return jnp.flip(t, axis=2)

    st_f, hc_f = mlstm_scan(sc[0], sc[1], sc[2], sc[3], sc[4], zero)
    _, hl_f = mlstm_scan(sl[0], sl[1], sl[2], sl[3], sl[4], st_f)
    st_b, hc_b = mlstm_scan(rev(sc[0]), rev(sc[1]), rev(sc[2]), rev(sc[5]), rev(sc[6]), zero)
    _, hl_b = mlstm_scan(rev(sl[0]), rev(sl[1]), rev(sl[2]), rev(sl[5]), rev(sl[6]), st_b)
    return hc_f + rev(hc_b), hl_f + rev(hl_b)


def mlstm_output(h, mo, mlstm_norm):
    hn = rmsnorm(h, mlstm_norm.reshape(MLSTM_HEADS, 1, MLSTM_DV))
    return merge_heads(hn * jax.nn.sigmoid(to_heads(mo, MLSTM_HEADS).astype(jnp.float32)))


def mixer_ab(z_lat, z_ctx, w_in, gate_bias, q_norm, k_norm, mlstm_norm, w_out, cos, sin):
    dt = z_lat.dtype
    aq_l, ak_l, av_l, mq_l, mk_l, mv_l, mo_l, mg_l = jnp.split(z_lat @ w_in, IN_SPLITS, axis=-1)
    aq_c, ak_c, av_c, mq_c, mk_c, mv_c, mo_c, mg_c = jnp.split(z_ctx @ w_in, IN_SPLITS, axis=-1)
    q_l = apply_rope(rmsnorm(to_heads(aq_l, ATTN_HEADS), q_norm), cos, sin)
    k_l = apply_rope(rmsnorm(to_heads(ak_l, ATTN_KV_HEADS), k_norm), cos, sin)
    q_c = rmsnorm(to_heads(aq_c, ATTN_HEADS), q_norm)
    k_c = rmsnorm(to_heads(ak_c, ATTN_KV_HEADS), k_norm)
    v_l = to_heads(av_l, ATTN_KV_HEADS)
    v_c = to_heads(av_c, ATTN_KV_HEADS)
    a_l = block_attention(q_l, jnp.concatenate([k_c, k_l], axis=2), jnp.concatenate([v_c, v_l], axis=2))
    a_c = block_attention(q_c, k_c, v_c)
    h_c, h_l = mlstm_bidirectional(mlstm_stream(mq_c, mk_c, mv_c, mg_c, gate_bias),
                                   mlstm_stream(mq_l, mk_l, mv_l, mg_l, gate_bias))
    m_l = mlstm_output(h_l, mo_l, mlstm_norm).astype(dt)
    m_c = mlstm_output(h_c, mo_c, mlstm_norm).astype(dt)
    y_l = jnp.concatenate([merge_heads(a_l), m_l], axis=-1) @ w_out
    y_c = jnp.concatenate([merge_heads(a_c), m_c], axis=-1) @ w_out
    return y_l, y_c


def pool_mixer(z, pool_w, pool_scale):
    b, s, d = z.shape
    zf = z.astype(jnp.float32)
    csum = jnp.concatenate([jnp.zeros((b, 1, d), jnp.float32), jnp.cumsum(zf, axis=1)], axis=1)
    t = jnp.arange(s)
    groups = []
    for gi, w in enumerate(POOL_WINDOWS):
        left = w // 2
        right = w - 1 - left
        lo = jnp.clip(t - left, 0, s)
        hi = jnp.clip(t + right + 1, 0, s)
        ch = slice(gi * POOL_GROUP, (gi + 1) * POOL_GROUP)
        mean = (csum[:, hi, ch] - csum[:, lo, ch]) / (hi - lo).astype(jnp.float32)[None, :, None]
        groups.append(mean - zf[:, :, ch])
    d_pool = jnp.stack(groups, axis=2).astype(z.dtype)
    y = jnp.einsum('bsgc,gce->bsge', d_pool, pool_w).reshape(b, s, d)
    return y * pool_scale


def moe(z, router_w, router_bias, exp_gate, exp_up, exp_down, sh_gate, sh_up, sh_down):
    b, s, d = z.shape
    zt = z.reshape(-1, d)
    n_tok = zt.shape[0]
    scores = jax.nn.sigmoid((zt @ router_w).astype(jnp.float32))
    sel = scores + router_bias.astype(jnp.float32)
    grp_score = jnp.sum(lax.top_k(sel.reshape(n_tok, N_GROUPS, N_EXPERTS // N_GROUPS), 2)[0], axis=-1)
    _, top_g = lax.top_k(grp_score, TOPK_GROUPS)
    gmask = jnp.sum(jax.nn.one_hot(top_g, N_GROUPS, dtype=jnp.float32), axis=-2)
    emask = jnp.repeat(gmask, N_EXPERTS // N_GROUPS, axis=-1) > 0
    sel = jnp.where(emask, sel, -jnp.inf)
    _, idx = lax.top_k(sel, TOP_K)
    w = jnp.take_along_axis(scores, idx, axis=-1)
    w = w / jnp.sum(w, axis=-1, keepdims=True) * ROUTED_SCALE
    gates = jnp.zeros((n_tok, N_EXPERTS), jnp.float32).at[jnp.arange(n_tok)[:, None], idx].set(w)
    hid = jax.nn.silu(jnp.einsum('td,edf->tef', zt, exp_gate)) * jnp.einsum('td,edf->tef', zt, exp_up)
    routed = jnp.einsum('tef,efd->td', hid * gates.astype(hid.dtype)[..., None], exp_down)
    shared = (jax.nn.silu(zt @ sh_gate) * (zt @ sh_up)) @ sh_down
    return (routed + shared).reshape(b, s, d)


def setup_inputs(seed: int = 0) -> dict:
    key = jax.random.key(seed)
    ks = jax.random.split(key, 24)
    D = D_MODEL

    def nrm(k, shape, scale):
        return jax.random.normal(k, shape, jnp.float32) * scale

    gate_offset = jnp.tile(jnp.repeat(jnp.array([0.0, 3.0], jnp.float32), MLSTM_HEADS), 2)
    return {
        'x': nrm(ks[0], (BATCH, SEQ, D), 1.0),
        'c': nrm(ks[1], (BATCH, D), 1.0),
        'ctx': nrm(ks[2], (BATCH, CTX_LEN, D), 1.0),
        'c_ctx': nrm(ks[3], (D,), 1.0),
        'ada_w': nrm(ks[4], (DEPTH, D, 6 * D), 0.5 * D ** -0.5),
        'ada_b': nrm(ks[5], (DEPTH, 6 * D), 0.02),
        'norm_mix': 1.0 + nrm(ks[6], (DEPTH, D), 0.02),
        'norm_ffn': 1.0 + nrm(ks[7], (DEPTH, D), 0.02),
        'w_in': nrm(ks[8], (N_EVEN, D, IN_COLS), D ** -0.5),
        'gate_bias': gate_offset + nrm(ks[9], (N_EVEN, 4 * MLSTM_HEADS), 0.1),
        'q_norm': 1.0 + nrm(ks[10], (N_EVEN, ATTN_HEAD_DIM), 0.02),
        'k_norm': 1.0 + nrm(ks[11], (N_EVEN, ATTN_HEAD_DIM), 0.02),
        'mlstm_norm': 1.0 + nrm(ks[12], (N_EVEN, MLSTM_HEADS * MLSTM_DV), 0.02),
        'w_out': nrm(ks[13], (N_EVEN, D, D), D ** -0.5),
        'pool_w': nrm(ks[14], (N_ODD, len(POOL_WINDOWS), POOL_GROUP, POOL_GROUP), POOL_GROUP ** -0.5),
        'pool_scale': 1.0 + nrm(ks[15], (N_ODD, D), 0.02),
        'router_w': nrm(ks[16], (DEPTH, D, N_EXPERTS), D ** -0.5),
        'router_bias': nrm(ks[17], (DEPTH, N_EXPERTS), 0.01),
        'exp_gate': nrm(ks[18], (DEPTH, N_EXPERTS, D, D_EXPERT), D ** -0.5),
        'exp_up': nrm(ks[19], (DEPTH, N_EXPERTS, D, D_EXPERT), D ** -0.5),
        'exp_down': nrm(ks[20], (DEPTH, N_EXPERTS, D_EXPERT, D), D_EXPERT ** -0.5),
        'sh_gate': nrm(ks[21], (DEPTH, D, D_EXPERT), D ** -0.5),
        'sh_up': nrm(ks[22], (DEPTH, D, D_EXPERT), D ** -0.5),
        'sh_down': nrm(ks[23], (DEPTH, D_EXPERT, D), D_EXPERT ** -0.5),
    }


def reference(x, c, ctx, c_ctx, ada_w, ada_b, norm_mix, norm_ffn, w_in, gate_bias, q_norm, k_norm,
              mlstm_norm, w_out, pool_w, pool_scale, router_w, router_bias, exp_gate, exp_up, exp_down,
              sh_gate, sh_up, sh_down):
    cos, sin = axial_rope_tables(x.shape[1])
    n_ctx = ctx.shape[1]
    silu_c = jax.nn.silu(c)
    silu_cc = jax.nn.silu(c_ctx)
    h_lat, h_ctx = x, ctx
    for layer in range(DEPTH):
        last = layer == DEPTH - 1
        even = layer % 2 == 0
        j = layer // 2
        need_ctx = even or not last
        mod_l = jnp.split((silu_c @ ada_w[layer] + ada_b[layer])[:, None, :], 6, axis=-1)
        z_lat = modulate(rmsnorm(h_lat, norm_mix[layer]), mod_l[0], mod_l[1])
        if need_ctx:
            mod_c = jnp.split((silu_cc @ ada_w[layer] + ada_b[layer])[None, None, :], 6, axis=-1)
            z_ctx = modulate(rmsnorm(h_ctx, norm_mix[layer]), mod_c[0], mod_c[1])
        if even:
            y_lat, y_ctx = mixer_ab(z_lat, z_ctx, w_in[j], gate_bias[j], q_norm[j], k_norm[j],
                                    mlstm_norm[j], w_out[j], cos, sin)
        else:
            y_lat = pool_mixer(z_lat, pool_w[j], pool_scale[j])
            if not last:
                y_ctx = pool_mixer(z_ctx, pool_w[j], pool_scale[j])
        h_lat = h_lat + jnp.tanh(mod_l[2]) * y_lat
        zl = modulate(rmsnorm(h_lat, norm_ffn[layer]), mod_l[3], mod_l[4])
        moe_args = (router_w[layer], router_bias[layer], exp_gate[layer], exp_up[layer], exp_down[layer],
                    sh_gate[layer], sh_up[layer], sh_down[layer])
        if last:
            h_lat = h_lat + jnp.tanh(mod_l[5]) * moe(zl, *moe_args)
        else:
            h_ctx = h_ctx + jnp.tanh(mod_c[2]) * y_ctx
            zc = modulate(rmsnorm(h_ctx, norm_ffn[layer]), mod_c[3], mod_c[4])
            m_all = moe(jnp.concatenate([zc, zl], axis=1), *moe_args)
            h_ctx = h_ctx + jnp.tanh(mod_c[5]) * m_all[:, :n_ctx]
            h_lat = h_lat + jnp.tanh(mod_l[5]) * m_all[:, n_ctx:]
    return h_lat
```

```python
import functools

import jax
import jax.numpy as jnp
from jax import lax
from jax.experimental import pallas as pl
from jax.experimental.pallas import tpu as pltpu

F32 = jnp.float32
BF16 = jnp.bfloat16

EPS = 1e-6
ATTN_HEAD_DIM = 128
GQA_GROUP = 4
ROPE_THETA = 10000.0
GRID_W = 64
MLSTM_HEADS = 4
GATE_SOFTCAP = 15.0
POOL_WINDOWS = (2, 4, 8, 16)
POOL_HALO = 16
TOP_K = 8
N_GROUPS = 8
TOPK_GROUPS = 4
ROUTED_SCALE = 2.5

LANES = 128
V7X_VMEM_BUDGET = 56 * 2**20
COMPILER_TEMP_BYTES = 8 * 2**20
ROW_TILE = 256
MLSTM_CHUNK = 256
ATTN_KV_CHUNK = 512
EXPERT_TILE = 256
COMBINE_TILE = 128


def _cparams(sem, vmem_bytes):
    limit = min(vmem_bytes + COMPILER_TEMP_BYTES, V7X_VMEM_BUDGET)
    return pltpu.CompilerParams(dimension_semantics=sem, vmem_limit_bytes=int(limit))


def _pick_tile(n, candidates):
    for c in candidates:
        if n % c == 0:
            return c
    raise ValueError(f"no tile for {n}")


ADA_K_CHUNK = 512


def _ada_kernel(s_ref, w_ref, b_ref, o_ref):
    d = w_ref.shape[1]
    tn = w_ref.shape[2]
    sub = 8

    def body(kc, acc):
        k0 = pl.multiple_of(kc * ADA_K_CHUNK, ADA_K_CHUNK)
        s = s_ref[pl.ds(k0, ADA_K_CHUNK), :]
        s = (s * jax.nn.sigmoid(s)).reshape(ADA_K_CHUNK // sub, sub, LANES)
        w = w_ref[0, pl.ds(k0, ADA_K_CHUNK), :].reshape(ADA_K_CHUNK // sub, sub, tn)
        return tuple(a + jnp.sum(w * s[:, :, j:j + 1], axis=0) for j, a in enumerate(acc))

    acc = lax.fori_loop(0, d // ADA_K_CHUNK, body, (jnp.zeros((sub, tn), F32), jnp.zeros((sub, tn), F32)))
    for j, a in enumerate(acc):
        o_ref[0, j:j + 1, :] = jnp.sum(a, axis=0, keepdims=True) + b_ref[0]


def ada_modulation(c, c_ctx, ada_w, ada_b):
    depth, d, n = ada_w.shape
    s = jnp.zeros((d, LANES), F32).at[:, 0].set(c[0]).at[:, 1].set(c_ctx)
    tn = _pick_tile(n, (1024, 512, 256, 128))
    out = pl.pallas_call(
        _ada_kernel,
        grid=(depth, n // tn),
        in_specs=[
            pl.BlockSpec((d, LANES), lambda l, j: (0, 0)),
            pl.BlockSpec((1, d, tn), lambda l, j: (l, 0, j)),
            pl.BlockSpec((1, 1, tn), lambda l, j: (l, 0, j)),
        ],
        out_specs=pl.BlockSpec((1, 2, tn), lambda l, j: (l, 0, j)),
        out_shape=jax.ShapeDtypeStruct((depth, 2, n), F32),
        compiler_params=_cparams(("arbitrary", "arbitrary"), 2 * d * tn * 4 + 8 * ADA_K_CHUNK * tn * 4),
        name="ada_mod",
    )(s, ada_w, ada_b.reshape(depth, 1, n))
    return out.reshape(depth, 2, 6, d)


def _stream_vec(mod_ref, idx, rows, n_ctx):
    lat = mod_ref[0, idx:idx + 1, :]
    if n_ctx == 0:
        return lat
    return jnp.where(rows < n_ctx, mod_ref[1, idx:idx + 1, :], lat)


def _norm_mod_kernel(h_ref, g_ref, mod_ref, *rest, n_ctx, tm, shift_idx, with_router):
    if with_router:
        rw_ref, z_ref, zp_ref, lg_ref = rest
    else:
        (z_ref,) = rest
    x = h_ref[...]
    rows = pl.program_id(0) * tm + lax.broadcasted_iota(jnp.int32, (tm, 1), 0)
    ms = jnp.mean(x * x, axis=-1, keepdims=True)
    y = x * lax.rsqrt(ms + EPS) * g_ref[...]
    z = y * (1.0 + _stream_vec(mod_ref, shift_idx + 1, rows, n_ctx)) + _stream_vec(mod_ref, shift_idx, rows, n_ctx)
    z_ref[...] = z.astype(BF16)
    if with_router:
        half = z.shape[1] // 2
        zp_ref[...] = pltpu.pack_elementwise([z[:, :half], z[:, half:]], packed_dtype=BF16)
        lg_ref[...] = jnp.dot(z, rw_ref[...], preferred_element_type=F32, precision=lax.Precision.HIGHEST)


def norm_modulate(h, g, mod, n_ctx, shift_idx, router_w=None):
    t, d = h.shape
    tm = ROW_TILE
    with_router = router_w is not None
    in_specs = [
        pl.BlockSpec((tm, d), lambda i: (i, 0)),
        pl.BlockSpec((1, d), lambda i: (0, 0)),
        pl.BlockSpec((2, 6, d), lambda i: (0, 0, 0)),
    ]
    args = [h, g.reshape(1, d), mod]
    out_specs = [pl.BlockSpec((tm, d), lambda i: (i, 0))]
    out_shape = [jax.ShapeDtypeStruct((t, d), BF16)]
    if with_router:
        e = router_w.shape[1]
        rw = jnp.zeros((d, LANES), F32).at[:, :e].set(router_w)
        in_specs.append(pl.BlockSpec((d, LANES), lambda i: (0, 0)))
        args.append(rw)
        out_specs += [pl.BlockSpec((tm, d // 2), lambda i: (i, 0)), pl.BlockSpec((tm, LANES), lambda i: (i, 0))]
        out_shape += [jax.ShapeDtypeStruct((t, d // 2), jnp.uint32), jax.ShapeDtypeStruct((t, LANES), F32)]
    outs = pl.pallas_call(
        functools.partial(_norm_mod_kernel, n_ctx=n_ctx, tm=tm, shift_idx=shift_idx, with_router=with_router),
        grid=(t // tm,),
        in_specs=in_specs,
        out_specs=out_specs,
        out_shape=out_shape,
        compiler_params=_cparams(("arbitrary",), 8 * tm * d * 4 + 4 * d * LANES * 4),
        name="norm_mod_router" if with_router else "norm_mod",
    )(*args)
    return outs if with_router else outs[0]


def _mm_kernel(x_ref, w_ref, o_ref):
    o_ref[...] = jnp.dot(x_ref[...], w_ref[...], preferred_element_type=F32).astype(o_ref.dtype)


def matmul(x, w, out_dtype):
    m, k = x.shape
    n = w.shape[1]
    tm = _pick_tile(m, (1056, 1024, 768, 640, 512, 256))
    tn = _pick_tile(n, (1024, 512, 256, 128))
    osz = jnp.dtype(out_dtype).itemsize
    return pl.pallas_call(
        _mm_kernel,
        grid=(n // tn, m // tm),
        in_specs=[pl.BlockSpec((tm, k), lambda j, i: (i, 0)), pl.BlockSpec((k, tn), lambda j, i: (0, j))],
        out_specs=pl.BlockSpec((tm, tn), lambda j, i: (i, j)),
        out_shape=jax.ShapeDtypeStruct((m, n), out_dtype),
        compiler_params=_cparams(("arbitrary", "arbitrary"),
                                 2 * (tm * k * 2 + k * tn * 2 + tm * tn * osz) + tm * tn * 4),
        name="matmul",
    )(x, w)


def _qk_prep_kernel(y_ref, cos_ref, sin_ref, qg_ref, kg_ref, o_ref, *, n_q, n_k):
    dh = ATTN_HEAD_DIM
    cos = cos_ref[...]
    sin = sin_ref[...]
    lane = lax.broadcasted_iota(jnp.int32, cos.shape, 1)
    first_half = (lane % (dh // 2)) < (dh // 4)
    q_scale = dh ** -0.5
    for hd in range(n_q + n_k):
        x = y_ref[:, hd * dh:(hd + 1) * dh].astype(F32)
        g = qg_ref[...] if hd < n_q else kg_ref[...]
        y = x * lax.rsqrt(jnp.mean(x * x, axis=-1, keepdims=True) + EPS) * g
        partner = jnp.where(first_half, pltpu.roll(y, dh - dh // 4, axis=1), pltpu.roll(y, dh // 4, axis=1))
        r = y * cos + partner * sin
        if hd < n_q:
            r = r * q_scale
        o_ref[:, hd * dh:(hd + 1) * dh] = r.astype(BF16)


def qk_prepare(y, cos_t, sin_t, q_norm, k_norm, n_q, n_k):
    t = y.shape[0]
    dh = ATTN_HEAD_DIM
    w = (n_q + n_k) * dh
    tm = ROW_TILE
    return pl.pallas_call(
        functools.partial(_qk_prep_kernel, n_q=n_q, n_k=n_k),
        grid=(t // tm,),
        in_specs=[
            pl.BlockSpec((tm, w), lambda i: (i, 0)),
            pl.BlockSpec((tm, dh), lambda i: (i, 0)),
            pl.BlockSpec((tm, dh), lambda i: (i, 0)),
            pl.BlockSpec((1, dh), lambda i: (0, 0)),
            pl.BlockSpec((1, dh), lambda i: (0, 0)),
        ],
        out_specs=pl.BlockSpec((tm, w), lambda i: (i, 0)),
        out_shape=jax.ShapeDtypeStruct((t, w), BF16),
        compiler_params=_cparams(("arbitrary",), 8 * tm * w * 4),
        name="qk_prep",
    )(y, cos_t, sin_t, q_norm.reshape(1, dh), k_norm.reshape(1, dh))


def rope_tables(n_ctx, seq):
    nf = ATTN_HEAD_DIM // 4
    rows = seq // GRID_W
    row = jnp.repeat(jnp.arange(rows, dtype=F32), GRID_W)
    col = jnp.tile(jnp.arange(GRID_W, dtype=F32), rows)
    inv_freq = ROPE_THETA ** (-jnp.arange(nf, dtype=F32) / nf)
    ang = jnp.stack([row, col], axis=-1)[..., None] * inv_freq
    cos, sin = jnp.cos(ang), jnp.sin(ang)
    cos_l = jnp.concatenate([cos, cos], axis=-1).reshape(seq, 4 * nf)
    sin_l = jnp.concatenate([-sin, sin], axis=-1).reshape(seq, 4 * nf)
    cos_t = jnp.concatenate([jnp.ones((n_ctx, 4 * nf), F32), cos_l], axis=0)
    sin_t = jnp.concatenate([jnp.zeros((n_ctx, 4 * nf), F32), sin_l], axis=0)
    return cos_t, sin_t


def _attn_kernel(q_ref, k_ref, v_ref, o_ref, m_sc, l_sc, acc_sc, *, n_ctx, kc, n_lat_chunks):
    dh = ATTN_HEAD_DIM
    grp = GQA_GROUP
    tq = q_ref.shape[0]
    q = jnp.concatenate([q_ref[:, g * dh:(g + 1) * dh] for g in range(grp)], axis=0)
    nt = (((1,), (1,)), ((), ()))

    s = lax.dot_general(q, k_ref[0:n_ctx, :], nt, preferred_element_type=F32)
    m0 = jnp.max(s, axis=-1, keepdims=True)
    p = jnp.exp(s - m0)
    m_sc[...] = m0
    l_sc[...] = jnp.sum(p, axis=-1, keepdims=True)
    acc_sc[...] = jnp.dot(p.astype(BF16), v_ref[0:n_ctx, :], preferred_element_type=F32)

    def body(c, carry):
        start = pl.multiple_of(n_ctx + c * kc, kc if n_ctx % kc == 0 else n_ctx)
        kk = k_ref[pl.ds(start, kc), :]
        vv = v_ref[pl.ds(start, kc), :]
        s = lax.dot_general(q, kk, nt, preferred_element_type=F32)
        m_prev = m_sc[...]
        m_new = jnp.maximum(m_prev, jnp.max(s, axis=-1, keepdims=True))
        alpha = jnp.exp(m_prev - m_new)
        p = jnp.exp(s - m_new)
        l_sc[...] = alpha * l_sc[...] + jnp.sum(p, axis=-1, keepdims=True)
        acc_sc[...] = alpha * acc_sc[...] + jnp.dot(p.astype(BF16), vv, preferred_element_type=F32)
        m_sc[...] = m_new
        return carry

    lax.fori_loop(0, jnp.where(pl.program_id(1) == 0, 0, n_lat_chunks), body, 0)
    out = acc_sc[...] / l_sc[...]
    for g in range(grp):
        o_ref[:, g * dh:(g + 1) * dh] = out[g * tq:(g + 1) * tq, :].astype(BF16)


def attention(qk, y, n_ctx, n_q, n_kv, v_col0):
    t = qk.shape[0]
    dh = ATTN_HEAD_DIM
    tq = n_ctx
    kc = _pick_tile(t - n_ctx, (ATTN_KV_CHUNK, 256, 128))
    rows = GQA_GROUP * tq
    return pl.pallas_call(
        functools.partial(_attn_kernel, n_ctx=n_ctx, kc=kc, n_lat_chunks=(t - n_ctx) // kc),
        grid=(n_kv, t // tq),
        in_specs=[
            pl.BlockSpec((tq, GQA_GROUP * dh), lambda h, i: (i, h)),
            pl.BlockSpec((t, dh), lambda h, i: (0, n_q + h)),
            pl.BlockSpec((t, dh), lambda h, i: (0, v_col0 // dh + h)),
        ],
        out_specs=pl.BlockSpec((tq, GQA_GROUP * dh), lambda h, i: (i, h)),
        out_shape=jax.ShapeDtypeStruct((t, n_q * dh), BF16),
        scratch_shapes=[pltpu.VMEM((rows, 1), F32), pltpu.VMEM((rows, 1), F32), pltpu.VMEM((rows, dh), F32)],
        compiler_params=_cparams(("arbitrary", "arbitrary"),
                                 4 * t * dh * 2 + 6 * rows * kc * 4 + 3 * rows * LANES * 4),
        name="attention",
    )(qk, qk, y)


def _mlstm_kernel(q_ref, k_ref, v_ref, g_ref, b_ref, h_ref, ct_sc, m_sc, *, n_heads, dk, dv):
    ln = q_ref.shape[0]
    dh_id = pl.program_id(0)
    direction = dh_id // n_heads
    head = dh_id % n_heads
    col_i = direction * 2 * n_heads + head
    col_f = col_i + n_heads

    @pl.when(pl.program_id(1) == 0)
    def _():
        ct_sc[...] = jnp.zeros_like(ct_sc)
        m_sc[...] = jnp.zeros_like(m_sc)

    pre = g_ref[...] + b_ref[...]
    pre = GATE_SOFTCAP * jnp.tanh(pre / GATE_SOFTCAP)
    lane = lax.broadcasted_iota(jnp.int32, pre.shape, 1)
    is_forget = ((lane // n_heads) % 2) == 1
    gates = jnp.where(is_forget, jax.nn.log_sigmoid(pre), pre)

    r = lax.broadcasted_iota(jnp.int32, (ln, ln), 0)
    c = lax.broadcasted_iota(jnp.int32, (ln, ln), 1)
    allowed = jnp.where(direction == 0, r - c, c - r) >= 0
    cum = jnp.dot(allowed.astype(F32), gates, preferred_element_type=F32, precision=lax.Precision.HIGHEST)

    def pick_col(a, idx):
        return jnp.sum(jnp.where(lane == idx, a, 0.0), axis=1, keepdims=True)

    sub = lax.broadcasted_iota(jnp.int32, (LANES, ln), 0)

    def pick_row(a, idx):
        return jnp.sum(jnp.where(sub == idx, a.T, 0.0), axis=0, keepdims=True)

    b_col = pick_col(cum, col_f)
    i_col = pick_col(gates, col_i)
    b_row = pick_row(cum, col_f)
    i_row = pick_row(gates, col_i)
    m_prev = m_sc[...]

    log_d = jnp.where(allowed, b_col - b_row + i_row, -jnp.inf)
    log_inter = b_col + m_prev
    m_t = jnp.maximum(log_inter, jnp.max(log_d, axis=1, keepdims=True))
    scale = dk ** -0.5
    d_mat = jnp.exp(log_d - m_t) * scale
    w_inter = jnp.exp(log_inter - m_t) * scale

    q = q_ref[...]
    k = k_ref[...]
    v_aug = jnp.concatenate([v_ref[...], jnp.ones((ln, LANES), BF16)], axis=1)
    s = lax.dot_general(q, k, (((1,), (1,)), ((), ())), preferred_element_type=F32) * d_mat
    intra = jnp.dot(s.astype(BF16), v_aug, preferred_element_type=F32)
    ct = ct_sc[...]
    inter = jnp.dot(q, ct.astype(BF16), preferred_element_type=F32)
    nd = intra + w_inter * inter
    den = nd[:, dv:dv + 1]
    h_ref[0] = nd[:, :dv] / jnp.maximum(jnp.abs(den), jnp.exp(-m_t))

    g_tot = jnp.sum(pick_col(gates, col_f), axis=0, keepdims=True)
    log_w = g_tot - b_col + i_col
    m_new = jnp.maximum(g_tot + m_prev, jnp.max(log_w, axis=0, keepdims=True))
    w = jnp.exp(log_w - m_new)
    decay = jnp.exp(g_tot + m_prev - m_new)
    wv = (w * v_aug.astype(F32)).astype(BF16)
    upd = lax.dot_general(k, wv, (((0,), (0,)), ((), ())), preferred_element_type=F32)
    ct_sc[...] = decay * ct + upd
    m_sc[...] = m_new


def mlstm(y, gates_raw, gate_bias, q_col0, k_col0, v_col0, dk, dv):
    t = y.shape[0]
    nh = MLSTM_HEADS
    ln = MLSTM_CHUNK
    nblk = t // ln
    bias = jnp.zeros((1, LANES), F32).at[0, :4 * nh].set(gate_bias)

    def blk(dhid, j):
        return jnp.where(dhid // nh == 0, j, jnp.where(j == 0, 0, nblk - j))

    return pl.pallas_call(
        functools.partial(_mlstm_kernel, n_heads=nh, dk=dk, dv=dv),
        grid=(2 * nh, nblk),
        in_specs=[
            pl.BlockSpec((ln, dk), lambda d, j: (blk(d, j), q_col0 // dk + d % nh)),
            pl.BlockSpec((ln, dk), lambda d, j: (blk(d, j), k_col0 // dk + d % nh)),
            pl.BlockSpec((ln, dv), lambda d, j: (blk(d, j), v_col0 // dv + d % nh)),
            pl.BlockSpec((ln, LANES), lambda d, j: (blk(d, j), 0)),
            pl.BlockSpec((1, LANES), lambda d, j: (0, 0)),
        ],
        out_specs=pl.BlockSpec((1, ln, dv), lambda d, j: (d // nh, blk(d, j), d % nh)),
        out_shape=jax.ShapeDtypeStruct((2, t, nh * dv), F32),
        scratch_shapes=[pltpu.VMEM((dk, dv + LANES), F32), pltpu.VMEM((1, 1), F32)],
        compiler_params=_cparams(("arbitrary", "arbitrary"), 32 * 2**20),
        name="mlstm_scan",
    )(y, y, y, gates_raw, bias)


def _mlstm_out_kernel(h_ref, mo_ref, g_ref, o_ref):
    h = h_ref[0] + h_ref[1]
    hn = h * lax.rsqrt(jnp.mean(h * h, axis=-1, keepdims=True) + EPS) * g_ref[...]
    o_ref[...] = (hn * jax.nn.sigmoid(mo_ref[...].astype(F32))).astype(BF16)


def mlstm_output(h2, y, mlstm_norm, mo_col0, dv):
    t = y.shape[0]
    nh = MLSTM_HEADS
    tm = ROW_TILE
    return pl.pallas_call(
        _mlstm_out_kernel,
        grid=(t // tm, nh),
        in_specs=[
            pl.BlockSpec((2, tm, dv), lambda i, h: (0, i, h)),
            pl.BlockSpec((tm, dv), lambda i, h: (i, mo_col0 // dv + h)),
            pl.BlockSpec((1, dv), lambda i, h: (0, h)),
        ],
        out_specs=pl.BlockSpec((tm, dv), lambda i, h: (i, h)),
        out_shape=jax.ShapeDtypeStruct((t, nh * dv), BF16),
        compiler_params=_cparams(("arbitrary", "arbitrary"), 16 * tm * dv * 4),
        name="mlstm_out",
    )(h2, y, mlstm_norm.reshape(1, nh * dv))


def _out_proj_kernel(a_ref, m_ref, w1_ref, w2_ref, h_ref, mod_ref, o_ref, *, n_ctx, tm, gate_idx):
    acc = jnp.dot(a_ref[...], w1_ref[...], preferred_element_type=F32)
    acc = acc + jnp.dot(m_ref[...], w2_ref[...], preferred_element_type=F32)
    rows = pl.program_id(1) * tm + lax.broadcasted_iota(jnp.int32, (tm, 1), 0)
    o_ref[...] = h_ref[...] + jnp.tanh(_stream_vec(mod_ref, gate_idx, rows, n_ctx)) * acc


def out_projection(a, m, w_out, h, mod, n_ctx, gate_idx):
    t, half = a.shape
    d = w_out.shape[1]
    tm = _pick_tile(t, (768, 512, 256))
    tn = _pick_tile(d, (1024, 512, 256, 128))
    return pl.pallas_call(
        functools.partial(_out_proj_kernel, n_ctx=n_ctx, tm=tm, gate_idx=gate_idx),
        grid=(d // tn, t // tm),
        in_specs=[
            pl.BlockSpec((tm, half), lambda j, i: (i, 0)),
            pl.BlockSpec((tm, half), lambda j, i: (i, 0)),
            pl.BlockSpec((half, tn), lambda j, i: (0, j)),
            pl.BlockSpec((half, tn), lambda j, i: (1, j)),
            pl.BlockSpec((tm, tn), lambda j, i: (i, j)),
            pl.BlockSpec((2, 6, tn), lambda j, i: (0, 0, j)),
        ],
        out_specs=pl.BlockSpec((tm, tn), lambda j, i: (i, j)),
        out_shape=jax.ShapeDtypeStruct((t, d), F32),
        compiler_params=_cparams(("arbitrary", "arbitrary"),
                                 2 * (2 * tm * half * 2 + 2 * half * tn * 2 + 2 * tm * tn * 4) + tm * tn * 4),
        name="out_proj",
    )(a, m, w_out, w_out, h, mod)


def _pool_kernel(h_ref, hp_ref, hn_ref, g_ref, mod_ref, o_ref, z_sc, *, seq, tm):
    i = pl.program_id(0)
    nblk = pl.num_programs(0)
    halo = POOL_HALO
    d = h_ref.shape[1]
    gd = d // len(POOL_WINDOWS)

    def normed(x):
        y = x * lax.rsqrt(jnp.mean(x * x, axis=-1, keepdims=True) + EPS) * g_ref[...]
        return y * (1.0 + mod_ref[0, 1:2, :]) + mod_ref[0, 0:1, :]

    z_sc[0:halo, :] = jnp.where(i > 0, normed(hp_ref[...]), 0.0)
    z_sc[halo:halo + tm, :] = normed(h_ref[...])
    z_sc[halo + tm:2 * halo + tm, :] = jnp.where(i < nblk - 1, normed(hn_ref[...]), 0.0)

    t = i * tm + lax.broadcasted_iota(jnp.int32, (tm, 1), 0)
    for gi, w in enumerate(POOL_WINDOWS):
        left = w // 2
        right = w - 1 - left
        cols = slice(gi * gd, (gi + 1) * gd)
        acc = z_sc[halo - left:halo - left + tm, cols]
        for off in range(-left + 1, right + 1):
            acc = acc + z_sc[halo + off:halo + off + tm, cols]
        cnt = (jnp.minimum(t + right + 1, seq) - jnp.maximum(t - left, 0)).astype(F32)
        o_ref[:, cols] = (acc / cnt - z_sc[halo:halo + tm, cols]).astype(BF16)


def pool_features(h, g, mod):
    seq, d = h.shape
    tm = ROW_TILE
    halo = POOL_HALO
    r = tm // halo
    nhb = seq // halo
    return pl.pallas_call(
        functools.partial(_pool_kernel, seq=seq, tm=tm),
        grid=(seq // tm,),
        in_specs=[
            pl.BlockSpec((tm, d), lambda i: (i, 0)),
            pl.BlockSpec((halo, d), lambda i: (jnp.maximum(i * r - 1, 0), 0)),
            pl.BlockSpec((halo, d), lambda i: (jnp.minimum((i + 1) * r, nhb - 1), 0)),
            pl.BlockSpec((1, d), lambda i: (0, 0)),
            pl.BlockSpec((2, 6, d), lambda i: (0, 0, 0)),
        ],
        out_specs=pl.BlockSpec((tm, d), lambda i: (i, 0)),
        out_shape=jax.ShapeDtypeStruct((seq, d), BF16),
        scratch_shapes=[pltpu.VMEM((tm + 2 * halo, d), F32)],
        compiler_params=_cparams(("arbitrary",), 10 * tm * d * 4),
        name="pool_features",
    )(h, h, h, g.reshape(1, d), mod)


def _pool_proj_kernel(x_ref, w_ref, ps_ref, h_ref, mod_ref, o_ref):
    acc = jnp.dot(x_ref[...], w_ref[0], preferred_element_type=F32)
    o_ref[...] = h_ref[...] + jnp.tanh(mod_ref[0, 2:3, :]) * (acc * ps_ref[...])


def pool_projection(dp, pool_w, pool_scale, h, mod):
    seq, d = h.shape
    ng, gd, _ = pool_w.shape
    tm = _pick_tile(seq, (1024, 512, 256))
    return pl.pallas_call(
        _pool_proj_kernel,
        grid=(ng, seq // tm),
        in_specs=[
            pl.BlockSpec((tm, gd), lambda g, i: (i, g)),
            pl.BlockSpec((1, gd, gd), lambda g, i: (g, 0, 0)),
            pl.BlockSpec((1, gd), lambda g, i: (0, g)),
            pl.BlockSpec((tm, gd), lambda g, i: (i, g)),
            pl.BlockSpec((2, 6, gd), lambda g, i: (0, 0, g)),
        ],
        out_specs=pl.BlockSpec((tm, gd), lambda g, i: (i, g)),
        out_shape=jax.ShapeDtypeStruct((seq, d), F32),
        compiler_params=_cparams(("arbitrary", "arbitrary"), 2 * (tm * gd * 10 + gd * gd * 2) + tm * gd * 4),
        name="pool_proj",
    )(dp, pool_w, pool_scale.reshape(1, d), h, mod)


def _route_kernel(lg_ref, bias_ref, gate_ref, idx_ref, rank_ref, cnt_ref, run_sc, *, n_exp):
    ng = N_GROUPS
    ge = n_exp // ng
    tt = lg_ref.shape[1]
    neg = -jnp.inf

    @pl.when(pl.program_id(0) == 0)
    def _():
        run_sc[...] = jnp.zeros_like(run_sc)

    scores = jax.nn.sigmoid(lg_ref[...])
    sel = (scores + bias_ref[...]).reshape(ng, ge, tt)
    e_in_g = lax.broadcasted_iota(jnp.int32, (ng, ge, tt), 1)
    m1 = jnp.max(sel, axis=1, keepdims=True)
    first = jnp.min(jnp.where(sel == m1, e_in_g, ge), axis=1, keepdims=True)
    m2 = jnp.max(jnp.where(e_in_g == first, neg, sel), axis=1, keepdims=True)
    gscore = m1 + m2

    gid = lax.broadcasted_iota(jnp.int32, (ng, 1, tt), 0)
    gmask = jnp.zeros((ng, 1, tt), jnp.bool_)
    for _ in range(TOPK_GROUPS):
        mx = jnp.max(gscore, axis=0, keepdims=True)
        pick = gid == jnp.min(jnp.where(gscore == mx, gid, ng), axis=0, keepdims=True)
        gmask = jnp.logical_or(gmask, pick)
        gscore = jnp.where(pick, neg, gscore)

    eid = lax.broadcasted_iota(jnp.int32, (ng, ge, tt), 0) * ge + e_in_g
    cand = jnp.where(gmask, sel, neg)
    chosen = jnp.zeros((ng, ge, tt), jnp.bool_)
    for kk in range(TOP_K):
        mx = jnp.max(jnp.max(cand, axis=1, keepdims=True), axis=0, keepdims=True)
        hit = jnp.where(cand == mx, eid, n_exp)
        pick_id = jnp.min(jnp.min(hit, axis=1, keepdims=True), axis=0, keepdims=True)
        pick = eid == pick_id
        chosen = jnp.logical_or(chosen, pick)
        cand = jnp.where(pick, neg, cand)
        idx_ref[kk:kk + 1, :] = pick_id.reshape(1, tt)

    s3 = scores.reshape(ng, ge, tt)
    w = jnp.where(chosen, s3, 0.0)
    denom = jnp.sum(jnp.sum(w, axis=1, keepdims=True), axis=0, keepdims=True)
    gate_ref[...] = (w / denom * ROUTED_SCALE).reshape(n_exp, tt)

    chosen_f = chosen.reshape(n_exp, tt).astype(F32)
    earlier = (lax.broadcasted_iota(jnp.int32, (tt, tt), 0) < lax.broadcasted_iota(jnp.int32, (tt, tt), 1))
    local = jnp.dot(chosen_f.astype(BF16), earlier.astype(BF16), preferred_element_type=F32)
    run = run_sc[...]
    rank_ref[...] = jnp.where(chosen.reshape(n_exp, tt), local + run, -1.0).astype(jnp.int32)
    run = run + jnp.sum(chosen_f, axis=1, keepdims=True)
    run_sc[...] = run
    cnt_ref[...] = jnp.broadcast_to(run, cnt_ref.shape).astype(jnp.int32)


def route(logits_t, router_bias):
    n_exp, t = logits_t.shape
    tt = _pick_tile(t, (512, 256, 128))
    gates, idx, rank, cnt = pl.pallas_call(
        functools.partial(_route_kernel, n_exp=n_exp),
        grid=(t // tt,),
        in_specs=[pl.BlockSpec((n_exp, tt), lambda i: (0, i)), pl.BlockSpec((n_exp, 1), lambda i: (0, 0))],
        out_specs=[
            pl.BlockSpec((n_exp, tt), lambda i: (0, i)),
            pl.BlockSpec((TOP_K, tt), lambda i: (0, i)),
            pl.BlockSpec((n_exp, tt), lambda i: (0, i)),
            pl.BlockSpec((n_exp, LANES), lambda i: (0, 0)),
        ],
        out_shape=[
            jax.ShapeDtypeStruct((n_exp, t), F32),
            jax.ShapeDtypeStruct((TOP_K, t), jnp.int32),
            jax.ShapeDtypeStruct((n_exp, t), jnp.int32),
            jax.ShapeDtypeStruct((n_exp, LANES), jnp.int32),
        ],
        scratch_shapes=[pltpu.VMEM((n_exp, 1), F32)],
        compiler_params=_cparams(("arbitrary",), 32 * 2**20),
        name="moe_route",
    )(logits_t, router_bias.reshape(n_exp, 1))
    return gates, idx, rank, cnt[:, 0]


def _row_copy(src, src_row, dst, dst_row, sem):
    return pltpu.make_async_copy(src.at[pl.ds(src_row, 1)], dst.at[pl.ds(dst_row, 1)], sem)


def _dispatch_kernel(pos_ref, zp_ref, xs_in_ref, xs_ref, sem, *, tt):
    del xs_in_ref
    base = pl.program_id(0) * tt

    def issue(n, carry):
        _row_copy(zp_ref, base + n // TOP_K, xs_ref, pos_ref[0, 0, n], sem).start()
        return carry

    lax.fori_loop(0, tt * TOP_K, issue, 0)

    def drain(n, carry):
        _row_copy(zp_ref, 0, xs_ref, 0, sem).wait()
        return carry

    lax.fori_loop(0, tt * TOP_K, drain, 0)


def dispatch(zp, pos_tk, n_rows):
    t, half = zp.shape
    tt = COMBINE_TILE
    xs0 = jnp.zeros((n_rows, half), jnp.uint32)
    return pl.pallas_call(
        functools.partial(_dispatch_kernel, tt=tt),
        grid=(t // tt,),
        in_specs=[
            pl.BlockSpec((1, 1, tt * TOP_K), lambda i: (i, 0, 0), memory_space=pltpu.SMEM),
            pl.BlockSpec(memory_space=pl.ANY),
            pl.BlockSpec(memory_space=pl.ANY),
        ],
        out_specs=pl.BlockSpec(memory_space=pl.ANY),
        out_shape=jax.ShapeDtypeStruct((n_rows, half), jnp.uint32),
        scratch_shapes=[pltpu.SemaphoreType.DMA(())],
        input_output_aliases={2: 0},
        compiler_params=_cparams(("arbitrary",), 16 * 2**20),
        name="moe_dispatch",
    )(pos_tk.reshape(t // tt, 1, tt * TOP_K), zp, xs0)


def _unpack_rows(xp):
    lo = pltpu.unpack_elementwise(xp, index=0, packed_dtype=BF16, unpacked_dtype=F32)
    hi = pltpu.unpack_elementwise(xp, index=1, packed_dtype=BF16, unpacked_dtype=F32)
    return jnp.concatenate([lo, hi], axis=1)


def _expert_kernel(te_ref, first_ref, nused_ref, x_ref, wg_ref, wu_ref, wd_ref, y_ref, wg_sc, wu_sc, wd_sc):
    i = pl.program_id(0)

    @pl.when(i < nused_ref[0])
    def _():
        @pl.when(first_ref[i] == 1)
        def _():
            wg_sc[...] = wg_ref[0].astype(BF16)
            wu_sc[...] = wu_ref[0].astype(BF16)
            wd_sc[...] = wd_ref[0].astype(BF16)

        x = _unpack_rows(x_ref[...]).astype(BF16)
        g = jnp.dot(x, wg_sc[...], preferred_element_type=F32)
        u = jnp.dot(x, wu_sc[...], preferred_element_type=F32)
        hid = (g * jax.nn.sigmoid(g) * u).astype(BF16)
        y = jnp.dot(hid, wd_sc[...], preferred_element_type=F32)
        half = y.shape[1] // 2
        y_ref[...] = pltpu.pack_elementwise([y[:, :half], y[:, half:]], packed_dtype=BF16)

    @pl.when(i >= nused_ref[0])
    def _():
        y_ref[...] = jnp.zeros_like(y_ref)


def expert_ffn(xs, tile_expert, tile_first, n_used, exp_gate, exp_up, exp_down):
    n_rows, half = xs.shape
    n_exp, d, f = exp_gate.shape
    tm = EXPERT_TILE
    n_tiles = n_rows // tm

    def row_blk(i, te, first, nused):
        return (jnp.minimum(i, nused[0] - 1), 0)

    def w_blk(i, te, first, nused):
        return (te[i], 0, 0)

    grid_spec = pltpu.PrefetchScalarGridSpec(
        num_scalar_prefetch=3,
        grid=(n_tiles,),
        in_specs=[
            pl.BlockSpec((tm, half), row_blk),
            pl.BlockSpec((1, d, f), w_blk),
            pl.BlockSpec((1, d, f), w_blk),
            pl.BlockSpec((1, f, d), w_blk),
        ],
        out_specs=pl.BlockSpec((tm, half), lambda i, te, first, nused: (i, 0)),
        scratch_shapes=[pltpu.VMEM((d, f), BF16), pltpu.VMEM((d, f), BF16), pltpu.VMEM((f, d), BF16)],
    )
    return pl.pallas_call(
        _expert_kernel,
        grid_spec=grid_spec,
        out_shape=jax.ShapeDtypeStruct((n_rows, half), jnp.uint32),
        compiler_params=_cparams(("arbitrary",), 2 * 3 * d * f * 4 + 3 * d * f * 2 + 4 * tm * half * 4
                                 + 6 * tm * d * 4),
        name="moe_experts",
    )(tile_expert, tile_first, n_used, xs, exp_gate, exp_up, exp_down)


def _combine_kernel(pos_ref, ys_ref, wt_ref, z_ref, sg_ref, su_ref, sd_ref, h_ref, mod_ref, o_ref, buf, sem,
                    *, tt, n_ctx, row0):
    def issue(n, carry):
        k = n % TOP_K
        _row_copy(ys_ref, pos_ref[0, 0, n], buf.at[k], n // TOP_K, sem).start()
        return carry

    lax.fori_loop(0, tt * TOP_K, issue, 0)

    z = z_ref[...]
    g = jnp.dot(z, sg_ref[...], preferred_element_type=F32)
    u = jnp.dot(z, su_ref[...], preferred_element_type=F32)
    acc = jnp.dot((g * jax.nn.sigmoid(g) * u).astype(BF16), sd_ref[...], preferred_element_type=F32)

    def drain(n, carry):
        _row_copy(ys_ref, 0, buf.at[0], 0, sem).wait()
        return carry

    lax.fori_loop(0, tt * TOP_K, drain, 0)

    wt = wt_ref[...]
    for k in range(TOP_K):
        acc = acc + wt[:, k:k + 1] * _unpack_rows(buf[k])
    rows = row0 + pl.program_id(0) * tt + lax.broadcasted_iota(jnp.int32, (tt, 1), 0)
    o_ref[...] = h_ref[...] + jnp.tanh(_stream_vec(mod_ref, 5, rows, n_ctx)) * acc


def combine(ys, pos_tk, w_tk, z, sh_gate, sh_up, sh_down, h, mod, n_ctx, row0):
    t, d = h.shape
    tt = COMBINE_TILE
    f = sh_gate.shape[1]
    half = d // 2
    b0 = row0 // tt
    n_out = t - row0
    return pl.pallas_call(
        functools.partial(_combine_kernel, tt=tt, n_ctx=n_ctx, row0=row0),
        grid=(n_out // tt,),
        in_specs=[
            pl.BlockSpec((1, 1, tt * TOP_K), lambda i: (b0 + i, 0, 0), memory_space=pltpu.SMEM),
            pl.BlockSpec(memory_space=pl.ANY),
            pl.BlockSpec((tt, TOP_K), lambda i: (b0 + i, 0)),
            pl.BlockSpec((tt, d), lambda i: (b0 + i, 0)),
            pl.BlockSpec((d, f), lambda i: (0, 0)),
            pl.BlockSpec((d, f), lambda i: (0, 0)),
            pl.BlockSpec((f, d), lambda i: (0, 0)),
            pl.BlockSpec((tt, d), lambda i: (b0 + i, 0)),
            pl.BlockSpec((2, 6, d), lambda i: (0, 0, 0)),
        ],
        out_specs=pl.BlockSpec((tt, d), lambda i: (i, 0)),
        out_shape=jax.ShapeDtypeStruct((n_out, d), F32),
        scratch_shapes=[pltpu.VMEM((TOP_K, tt, half), jnp.uint32), pltpu.SemaphoreType.DMA(())],
        compiler_params=_cparams(("arbitrary",), TOP_K * tt * half * 4 + 12 * tt * d * 4 + 6 * d * f * 2 * 2),
        name="moe_combine",
    )(pos_tk.reshape(t // tt, 1, tt * TOP_K), ys, w_tk, z, sh_gate, sh_up, sh_down, h, mod)


def moe_block(h, norm_g, mod, n_ctx, row0, router_w, router_bias, exp_gate, exp_up, exp_down, sh_gate, sh_up, sh_down):
    t, d = h.shape
    n_exp = router_w.shape[1]
    z, zp, logits = norm_modulate(h, norm_g, mod, n_ctx, 3, router_w=router_w)
    gates, idx_t, rank, counts = route(logits[:, :n_exp].T, router_bias)

    tm = EXPERT_TILE
    padded = (counts + tm - 1) // tm * tm
    ends = jnp.cumsum(padded)
    starts = ends - padded
    n_tiles = (t * TOP_K) // tm + n_exp
    tile_start = jnp.arange(n_tiles, dtype=jnp.int32) * tm
    n_used = (ends[-1] // tm).astype(jnp.int32)
    tile_expert = jnp.sum((ends[None, :] <= tile_start[:, None]).astype(jnp.int32), axis=1)
    tile_expert = jnp.minimum(tile_expert, n_exp - 1)
    tile_expert = jnp.where(tile_start < ends[-1], tile_expert, tile_expert[jnp.maximum(n_used - 1, 0)])
    tile_first = jnp.concatenate([jnp.ones((1,), jnp.int32), (tile_expert[1:] != tile_expert[:-1]).astype(jnp.int32)])

    pos_dense = starts[:, None] + rank
    pos_tk = jnp.take_along_axis(pos_dense, idx_t, axis=0).T.astype(jnp.int32)
    w_tk = jnp.take_along_axis(gates, idx_t, axis=0).T

    xs = dispatch(zp, pos_tk, n_tiles * tm)
    ys = expert_ffn(xs, tile_expert, tile_first, n_used.reshape(1), exp_gate, exp_up, exp_down)
    return combine(ys, pos_tk, w_tk, z, sh_gate.astype(BF16), sh_up.astype(BF16), sh_down.astype(BF16),
                   h, mod, n_ctx, row0)


def mixer_layer(h, n_ctx, mod, norm_g, w_in, gate_bias, q_norm, k_norm, mlstm_norm, w_out, cos_t, sin_t):
    t, d = h.shape
    dh = ATTN_HEAD_DIM
    n_q = (d // 2) // dh
    n_kv = n_q // GQA_GROUP
    nh = MLSTM_HEADS
    dv = (d // 2) // nh
    dk = dv // 2
    sizes = (n_q * dh, n_kv * dh, n_kv * dh, nh * dk, nh * dk, nh * dv, nh * dv, 4 * nh)
    off = [0]
    for s in sizes:
        off.append(off[-1] + s)
    n_main = off[7]

    z = norm_modulate(h, norm_g, mod, n_ctx, 0)
    y = matmul(z, w_in[:, :n_main].astype(BF16), BF16)
    wg = jnp.zeros((d, LANES), BF16).at[:, :4 * nh].set(w_in[:, n_main:].astype(BF16))
    gates_raw = matmul(z, wg, F32)

    qk = qk_prepare(y, cos_t, sin_t, q_norm, k_norm, n_q, n_kv)
    a = attention(qk, y, n_ctx, n_q, n_kv, off[2])
    h2 = mlstm(y, gates_raw, gate_bias, off[3], off[4], off[5], dk, dv)
    m = mlstm_output(h2, y, mlstm_norm, off[6], dv)
    return out_projection(a, m, w_out.astype(BF16), h, mod, n_ctx, 2)


def kernel(x, c, ctx, c_ctx, ada_w, ada_b, norm_mix, norm_ffn, w_in, gate_bias, q_norm, k_norm, mlstm_norm, w_out,
           pool_w, pool_scale, router_w, router_bias, exp_gate, exp_up, exp_down, sh_gate, sh_up, sh_down):
    depth = ada_w.shape[0]
    seq = x.shape[1]
    n_ctx = ctx.shape[1]
    assert x.shape[0] == 1 and n_ctx == ROW_TILE and seq % ROW_TILE == 0
    mods = ada_modulation(c, c_ctx, ada_w, ada_b)
    cos_t, sin_t = rope_tables(n_ctx, seq)

    h = jnp.concatenate([ctx[0], x[0]], axis=0)
    nc = n_ctx
    for layer in range(depth):
        last = layer == depth - 1
        j = layer // 2
        mod = mods[layer]
        if layer % 2 == 0:
            h = mixer_layer(h, nc, mod, norm_mix[layer], w_in[j], gate_bias[j], q_norm[j], k_norm[j],
                            mlstm_norm[j], w_out[j], cos_t, sin_t)
        else:
            assert nc == 0 or not last
            dp = pool_features(h, norm_mix[layer], mod)
            h = pool_projection(dp, pool_w[j].astype(BF16), pool_scale[j], h, mod)
        drop_ctx = nc > 0 and (last or (layer + 1 == depth - 1 and (depth - 1) % 2 == 1))
        row0 = nc if drop_ctx else 0
        h = moe_block(h, norm_ffn[layer], mod, nc, row0, router_w[layer], router_bias[layer], exp_gate[layer],
                      exp_up[layer], exp_down[layer], sh_gate[layer], sh_up[layer], sh_down[layer])
        if drop_ctx:
            nc = 0
    return h[nc:][None]
```

```python
import functools

import jax
import jax.numpy as jnp
from jax import lax
from jax.experimental import pallas as pl
from jax.experimental.pallas import tpu as pltpu

F32 = jnp.float32
BF16 = jnp.bfloat16

EPS = 1e-6
ATTN_HEAD_DIM = 128
GQA_GROUP = 4
ROPE_THETA = 10000.0
GRID_W = 64
MLSTM_HEADS = 4
GATE_SOFTCAP = 15.0
POOL_WINDOWS = (2, 4, 8, 16)
POOL_HALO = 16
TOP_K = 8
N_GROUPS = 8
TOPK_GROUPS = 4
ROUTED_SCALE = 2.5

LANES = 128
V7X_VMEM_BUDGET = 56 * 2**20
COMPILER_TEMP_BYTES = 8 * 2**20
ROW_TILE = 256
MLSTM_CHUNK = 256
ATTN_KV_CHUNK = 512
EXPERT_TILE = 256
COMBINE_TILE = 128


def _cparams(sem, vmem_bytes):
    limit = min(vmem_bytes + COMPILER_TEMP_BYTES, V7X_VMEM_BUDGET)
    return pltpu.CompilerParams(dimension_semantics=sem, vmem_limit_bytes=int(limit))


def _pick_tile(n, candidates):
    for c in candidates:
        if n % c == 0:
            return c
    raise ValueError(f"no tile for {n}")


ADA_K_CHUNK = 512


def _ada_kernel(s_ref, w_ref, b_ref, o_ref):
    d = w_ref.shape[1]
    tn = w_ref.shape[2]
    sub = 8

    def body(kc, acc):
        k0 = pl.multiple_of(kc * ADA_K_CHUNK, ADA_K_CHUNK)
        s = s_ref[pl.ds(k0, ADA_K_CHUNK), :]
        s = (s * jax.nn.sigmoid(s)).reshape(ADA_K_CHUNK // sub, sub, LANES)
        w = w_ref[0, pl.ds(k0, ADA_K_CHUNK), :].reshape(ADA_K_CHUNK // sub, sub, tn)
        return tuple(a + jnp.sum(w * s[:, :, j:j + 1], axis=0) for j, a in enumerate(acc))

    acc = lax.fori_loop(0, d // ADA_K_CHUNK, body, (jnp.zeros((sub, tn), F32), jnp.zeros((sub, tn), F32)))
    for j, a in enumerate(acc):
        o_ref[0, j:j + 1, :] = jnp.sum(a, axis=0, keepdims=True) + b_ref[0]


def ada_modulation(c, c_ctx, ada_w, ada_b):
    depth, d, n = ada_w.shape
    s = jnp.zeros((d, LANES), F32).at[:, 0].set(c[0]).at[:, 1].set(c_ctx)
    tn = _pick_tile(n, (1024, 512, 256, 128))
    out = pl.pallas_call(
        _ada_kernel,
        grid=(depth, n // tn),
        in_specs=[
            pl.BlockSpec((d, LANES), lambda l, j: (0, 0)),
            pl.BlockSpec((1, d, tn), lambda l, j: (l, 0, j)),
            pl.BlockSpec((1, 1, tn), lambda l, j: (l, 0, j)),
        ],
        out_specs=pl.BlockSpec((1, 2, tn), lambda l, j: (l, 0, j)),
        out_shape=jax.ShapeDtypeStruct((depth, 2, n), F32),
        compiler_params=_cparams(("arbitrary", "arbitrary"), 2 * d * tn * 4 + 8 * ADA_K_CHUNK * tn * 4),
        name="ada_mod",
    )(s, ada_w, ada_b.reshape(depth, 1, n))
    return out.reshape(depth, 2, 6, d)


def _stream_vec(mod_ref, idx, rows, n_ctx):
    lat = mod_ref[0, idx:idx + 1, :]
    if n_ctx == 0:
        return lat
    return jnp.where(rows < n_ctx, mod_ref[1, idx:idx + 1, :], lat)


def _pack_rows(x):
    half = x.shape[1] // 2
    return pltpu.pack_elementwise([x[:, :half], x[:, half:]], packed_dtype=BF16)


def _unpack_rows(xp):
    lo = pltpu.unpack_elementwise(xp, index=0, packed_dtype=BF16, unpacked_dtype=F32)
    hi = pltpu.unpack_elementwise(xp, index=1, packed_dtype=BF16, unpacked_dtype=F32)
    return jnp.concatenate([lo, hi], axis=1)


def _store_slabs(ref, xp):
    for c in range(ref.shape[1]):
        ref[:, c, :] = xp[:, c * LANES:(c + 1) * LANES]


def _load_slabs(ref):
    return jnp.concatenate([ref[:, c, :] for c in range(ref.shape[1])], axis=1)


def _norm_mod_kernel(h_ref, g_ref, mod_ref, *rest, n_ctx, tm, shift_idx, with_router):
    if with_router:
        rw_ref, z_ref, zp_ref, lg_ref = rest
    else:
        (z_ref,) = rest
    x = h_ref[...]
    rows = pl.program_id(0) * tm + lax.broadcasted_iota(jnp.int32, (tm, 1), 0)
    ms = jnp.mean(x * x, axis=-1, keepdims=True)
    y = x * lax.rsqrt(ms + EPS) * g_ref[...]
    z = y * (1.0 + _stream_vec(mod_ref, shift_idx + 1, rows, n_ctx)) + _stream_vec(mod_ref, shift_idx, rows, n_ctx)
    z_ref[...] = z.astype(BF16)
    if with_router:
        _store_slabs(zp_ref, _pack_rows(z))
        lg_ref[...] = jnp.dot(z, rw_ref[...], preferred_element_type=F32, precision=lax.Precision.HIGHEST)


def norm_modulate(h, g, mod, n_ctx, shift_idx, router_w=None):
    t, d = h.shape
    tm = ROW_TILE
    with_router = router_w is not None
    in_specs = [
        pl.BlockSpec((tm, d), lambda i: (i, 0)),
        pl.BlockSpec((1, d), lambda i: (0, 0)),
        pl.BlockSpec((2, 6, d), lambda i: (0, 0, 0)),
    ]
    args = [h, g.reshape(1, d), mod]
    out_specs = [pl.BlockSpec((tm, d), lambda i: (i, 0))]
    out_shape = [jax.ShapeDtypeStruct((t, d), BF16)]
    if with_router:
        e = router_w.shape[1]
        rw = jnp.zeros((d, LANES), F32).at[:, :e].set(router_w)
        in_specs.append(pl.BlockSpec((d, LANES), lambda i: (0, 0)))
        args.append(rw)
        ns = d // 2 // LANES
        out_specs += [pl.BlockSpec((tm, ns, LANES), lambda i: (i, 0, 0)), pl.BlockSpec((tm, LANES), lambda i: (i, 0))]
        out_shape += [jax.ShapeDtypeStruct((t, ns, LANES), jnp.uint32), jax.ShapeDtypeStruct((t, LANES), F32)]
    outs = pl.pallas_call(
        functools.partial(_norm_mod_kernel, n_ctx=n_ctx, tm=tm, shift_idx=shift_idx, with_router=with_router),
        grid=(t // tm,),
        in_specs=in_specs,
        out_specs=out_specs,
        out_shape=out_shape,
        compiler_params=_cparams(("arbitrary",), 8 * tm * d * 4 + 4 * d * LANES * 4),
        name="norm_mod_router" if with_router else "norm_mod",
    )(*args)
    return outs if with_router else outs[0]


def _mm_kernel(x_ref, w_ref, o_ref, wb_sc):
    @pl.when(pl.program_id(1) == 0)
    def _():
        wb_sc[...] = w_ref[...].astype(BF16)

    o_ref[...] = jnp.dot(x_ref[...], wb_sc[...], preferred_element_type=F32).astype(o_ref.dtype)


def matmul(x, w, n, out_dtype):
    m, k = x.shape
    tm = _pick_tile(m, (1056, 1024, 768, 640, 512, 256))
    tn = _pick_tile(n, (512, 256, 128))
    osz = jnp.dtype(out_dtype).itemsize
    wsz = jnp.dtype(w.dtype).itemsize
    return pl.pallas_call(
        _mm_kernel,
        grid=(n // tn, m // tm),
        in_specs=[pl.BlockSpec((tm, k), lambda j, i: (i, 0)), pl.BlockSpec((k, tn), lambda j, i: (0, j))],
        out_specs=pl.BlockSpec((tm, tn), lambda j, i: (i, j)),
        out_shape=jax.ShapeDtypeStruct((m, n), out_dtype),
        scratch_shapes=[pltpu.VMEM((k, tn), BF16)],
        compiler_params=_cparams(("arbitrary", "arbitrary"),
                                 2 * (tm * k * 2 + k * tn * wsz + tm * tn * osz) + k * tn * 2 + tm * tn * 4),
        name="matmul",
    )(x, w)


LOG2_E = 1.4426950408889634


def _qk_prep_kernel(y_ref, cos_ref, sin_ref, qg_ref, kg_ref, qt_ref, k_ref, vt_ref, *, n_q, n_k):
    dh = ATTN_HEAD_DIM
    tq = y_ref.shape[0]
    cos = cos_ref[...]
    sin = sin_ref[...]
    lane = lax.broadcasted_iota(jnp.int32, cos.shape, 1)
    first_half = (lane % (dh // 2)) < (dh // 4)
    q_scale = dh ** -0.5 * LOG2_E
    for hd in range(n_q + n_k):
        x = y_ref[:, hd * dh:(hd + 1) * dh].astype(F32)
        g = qg_ref[...] if hd < n_q else kg_ref[...]
        y = x * lax.rsqrt(jnp.mean(x * x, axis=-1, keepdims=True) + EPS) * g
        partner = jnp.where(first_half, pltpu.roll(y, dh - dh // 4, axis=1), pltpu.roll(y, dh // 4, axis=1))
        r = y * cos + partner * sin
        if hd < n_q:
            kv, g_in = divmod(hd, GQA_GROUP)
            qt_ref[kv, 0, :, g_in * tq:(g_in + 1) * tq] = (r * q_scale).T.astype(BF16)
        else:
            k_ref[:, (hd - n_q) * dh:(hd - n_q + 1) * dh] = r.astype(BF16)
    for kv in range(n_k):
        v = y_ref[:, (n_q + n_k + kv) * dh:(n_q + n_k + kv + 1) * dh].astype(F32)
        vt_ref[kv, 0] = v.T.astype(BF16)


def qk_prepare(y, cos_t, sin_t, q_norm, k_norm, n_q, n_k):
    t = y.shape[0]
    dh = ATTN_HEAD_DIM
    w = (n_q + 2 * n_k) * dh
    tm = ROW_TILE
    return pl.pallas_call(
        functools.partial(_qk_prep_kernel, n_q=n_q, n_k=n_k),
        grid=(t // tm,),
        in_specs=[
            pl.BlockSpec((tm, w), lambda i: (i, 0)),
            pl.BlockSpec((tm, dh), lambda i: (i, 0)),
            pl.BlockSpec((tm, dh), lambda i: (i, 0)),
            pl.BlockSpec((1, dh), lambda i: (0, 0)),
            pl.BlockSpec((1, dh), lambda i: (0, 0)),
        ],
        out_specs=[
            pl.BlockSpec((n_k, 1, dh, GQA_GROUP * tm), lambda i: (0, i, 0, 0)),
            pl.BlockSpec((tm, n_k * dh), lambda i: (i, 0)),
            pl.BlockSpec((n_k, 1, dh, tm), lambda i: (0, i, 0, 0)),
        ],
        out_shape=[
            jax.ShapeDtypeStruct((n_k, t // tm, dh, GQA_GROUP * tm), BF16),
            jax.ShapeDtypeStruct((t, n_k * dh), BF16),
            jax.ShapeDtypeStruct((n_k, t // tm, dh, tm), BF16),
        ],
        compiler_params=_cparams(("arbitrary",), 10 * tm * w * 4),
        name="qk_prep",
    )(y, cos_t, sin_t, q_norm.reshape(1, dh), k_norm.reshape(1, dh))


def rope_tables(n_ctx, seq):
    nf = ATTN_HEAD_DIM // 4
    rows = seq // GRID_W
    row = jnp.repeat(jnp.arange(rows, dtype=F32), GRID_W)
    col = jnp.tile(jnp.arange(GRID_W, dtype=F32), rows)
    inv_freq = ROPE_THETA ** (-jnp.arange(nf, dtype=F32) / nf)
    ang = jnp.stack([row, col], axis=-1)[..., None] * inv_freq
    cos, sin = jnp.cos(ang), jnp.sin(ang)
    cos_l = jnp.concatenate([cos, cos], axis=-1).reshape(seq, 4 * nf)
    sin_l = jnp.concatenate([-sin, sin], axis=-1).reshape(seq, 4 * nf)
    cos_t = jnp.concatenate([jnp.ones((n_ctx, 4 * nf), F32), cos_l], axis=0)
    sin_t = jnp.concatenate([jnp.zeros((n_ctx, 4 * nf), F32), sin_l], axis=0)
    return cos_t, sin_t


def _attn_kernel(qt_ref, k_ref, vt_ref, o_ref, m_sc, l_sc, acc_sc, *, n_ctx, kc, n_lat_chunks):
    dh = ATTN_HEAD_DIM
    tq = o_ref.shape[0]
    qt = qt_ref[0, 0]

    s = jnp.dot(k_ref[0:n_ctx, :], qt, preferred_element_type=F32)
    m0 = jnp.max(s, axis=0, keepdims=True)
    p = jnp.exp2(s - m0)
    m_sc[...] = m0
    l_sc[...] = jnp.sum(p, axis=0, keepdims=True)
    acc_sc[...] = jnp.dot(vt_ref[0, 0], p.astype(BF16), preferred_element_type=F32)
    tiles = kc // n_ctx

    def body(c, carry):
        start = pl.multiple_of(n_ctx + c * kc, n_ctx)
        s = jnp.dot(k_ref[pl.ds(start, kc), :], qt, preferred_element_type=F32)
        m_prev = m_sc[...]
        m_new = jnp.maximum(m_prev, jnp.max(s, axis=0, keepdims=True))
        alpha = jnp.exp2(m_prev - m_new)
        p = jnp.exp2(s - m_new)
        l_sc[...] = alpha * l_sc[...] + jnp.sum(p, axis=0, keepdims=True)
        vt = jnp.concatenate([vt_ref[0, 1 + c * tiles + j] for j in range(tiles)], axis=1)
        pv = jnp.dot(vt, p.astype(BF16), preferred_element_type=F32)
        acc_sc[...] = alpha * acc_sc[...] + pv
        m_sc[...] = m_new
        return carry

    lax.fori_loop(0, jnp.where(pl.program_id(1) == 0, 0, n_lat_chunks), body, 0)
    out = acc_sc[...] / l_sc[...]
    for g in range(GQA_GROUP):
        o_ref[:, g * dh:(g + 1) * dh] = out[:, g * tq:(g + 1) * tq].T.astype(BF16)


def attention(qt, k, vt, n_ctx):
    n_kv, nblk, dh, cols = qt.shape
    t = k.shape[0]
    tq = cols // GQA_GROUP
    assert tq == n_ctx
    kc = _pick_tile(t - n_ctx, (ATTN_KV_CHUNK, n_ctx))
    return pl.pallas_call(
        functools.partial(_attn_kernel, n_ctx=n_ctx, kc=kc, n_lat_chunks=(t - n_ctx) // kc),
        grid=(n_kv, nblk),
        in_specs=[
            pl.BlockSpec((1, 1, dh, cols), lambda h, i: (h, i, 0, 0)),
            pl.BlockSpec((t, dh), lambda h, i: (0, h)),
            pl.BlockSpec((1, nblk, dh, tq), lambda h, i: (h, 0, 0, 0)),
        ],
        out_specs=pl.BlockSpec((tq, GQA_GROUP * dh), lambda h, i: (i, h)),
        out_shape=jax.ShapeDtypeStruct((t, n_kv * GQA_GROUP * dh), BF16),
        scratch_shapes=[pltpu.VMEM((1, cols), F32), pltpu.VMEM((1, cols), F32), pltpu.VMEM((dh, cols), F32)],
        compiler_params=_cparams(("arbitrary", "arbitrary"), 4 * t * dh * 2 + 6 * kc * cols * 4 + 4 * dh * cols * 4),
        name="attention",
    )(qt, k, vt)


def _mlstm_kernel(q_ref, k_ref, v_ref, g_ref, b_ref, h_ref, ct_sc, m_sc, *, n_heads, dk, dv):
    ln = q_ref.shape[0]
    dh_id = pl.program_id(0)
    direction = dh_id // n_heads
    head = dh_id % n_heads
    col_i = direction * 2 * n_heads + head
    col_f = col_i + n_heads

    @pl.when(pl.program_id(1) == 0)
    def _():
        ct_sc[...] = jnp.zeros_like(ct_sc)
        m_sc[...] = jnp.zeros_like(m_sc)

    pre = g_ref[...] + b_ref[...]
    pre = GATE_SOFTCAP * jnp.tanh(pre / GATE_SOFTCAP)
    lane = lax.broadcasted_iota(jnp.int32, pre.shape, 1)
    is_forget = ((lane // n_heads) % 2) == 1
    gates = jnp.where(is_forget, jax.nn.log_sigmoid(pre), pre)

    r = lax.broadcasted_iota(jnp.int32, (ln, ln), 0)
    c = lax.broadcasted_iota(jnp.int32, (ln, ln), 1)
    allowed = jnp.where(direction == 0, r - c, c - r) >= 0
    cum = jnp.dot(allowed.astype(F32), gates, preferred_element_type=F32, precision=lax.Precision.HIGHEST)

    def pick_col(a, idx):
        return jnp.sum(jnp.where(lane == idx, a, 0.0), axis=1, keepdims=True)

    sub = lax.broadcasted_iota(jnp.int32, (LANES, ln), 0)

    def pick_row(a, idx):
        return jnp.sum(jnp.where(sub == idx, a.T, 0.0), axis=0, keepdims=True)

    b_col = pick_col(cum, col_f)
    i_col = pick_col(gates, col_i)
    b_row = pick_row(cum, col_f)
    i_row = pick_row(gates, col_i)
    m_prev = m_sc[...]

    log_d = jnp.where(allowed, b_col - b_row + i_row, -jnp.inf)
    log_inter = b_col + m_prev
    m_t = jnp.maximum(log_inter, jnp.max(log_d, axis=1, keepdims=True))
    scale = dk ** -0.5
    d_mat = jnp.exp(log_d - m_t) * scale
    w_inter = jnp.exp(log_inter - m_t) * scale

    q = q_ref[...]
    k = k_ref[...]
    v_aug = jnp.concatenate([v_ref[...], jnp.ones((ln, LANES), BF16)], axis=1)
    s = lax.dot_general(q, k, (((1,), (1,)), ((), ())), preferred_element_type=F32) * d_mat
    intra = jnp.dot(s.astype(BF16), v_aug, preferred_element_type=F32)
    ct = ct_sc[...]
    inter = jnp.dot(q, ct.astype(BF16), preferred_element_type=F32)
    nd = intra + w_inter * inter
    den = nd[:, dv:dv + 1]
    h_ref[0] = nd[:, :dv] / jnp.maximum(jnp.abs(den), jnp.exp(-m_t))

    g_tot = jnp.sum(pick_col(gates, col_f), axis=0, keepdims=True)
    log_w = g_tot - b_col + i_col
    m_new = jnp.maximum(g_tot + m_prev, jnp.max(log_w, axis=0, keepdims=True))
    w = jnp.exp(log_w - m_new)
    decay = jnp.exp(g_tot + m_prev - m_new)
    wv = (w * v_aug.astype(F32)).astype(BF16)
    upd = lax.dot_general(k, wv, (((0,), (0,)), ((), ())), preferred_element_type=F32)
    ct_sc[...] = decay * ct + upd
    m_sc[...] = m_new


def mlstm(y, gates_raw, gate_bias, q_col0, k_col0, v_col0, dk, dv):
    t = y.shape[0]
    nh = MLSTM_HEADS
    ln = MLSTM_CHUNK
    nblk = t // ln
    bias = jnp.zeros((1, LANES), F32).at[0, :4 * nh].set(gate_bias)

    def blk(dhid, j):
        return jnp.where(dhid // nh == 0, j, jnp.where(j == 0, 0, nblk - j))

    return pl.pallas_call(
        functools.partial(_mlstm_kernel, n_heads=nh, dk=dk, dv=dv),
        grid=(2 * nh, nblk),
        in_specs=[
            pl.BlockSpec((ln, dk), lambda d, j: (blk(d, j), q_col0 // dk + d % nh)),
            pl.BlockSpec((ln, dk), lambda d, j: (blk(d, j), k_col0 // dk + d % nh)),
            pl.BlockSpec((ln, dv), lambda d, j: (blk(d, j), v_col0 // dv + d % nh)),
            pl.BlockSpec((ln, LANES), lambda d, j: (blk(d, j), 0)),
            pl.BlockSpec((1, LANES), lambda d, j: (0, 0)),
        ],
        out_specs=pl.BlockSpec((1, ln, dv), lambda d, j: (d // nh, blk(d, j), d % nh)),
        out_shape=jax.ShapeDtypeStruct((2, t, nh * dv), F32),
        scratch_shapes=[pltpu.VMEM((dk, dv + LANES), F32), pltpu.VMEM((1, 1), F32)],
        compiler_params=_cparams(("arbitrary", "arbitrary"), 32 * 2**20),
        name="mlstm_scan",
    )(y, y, y, gates_raw, bias)


def _mlstm_out_kernel(h_ref, mo_ref, g_ref, o_ref):
    h = h_ref[0] + h_ref[1]
    hn = h * lax.rsqrt(jnp.mean(h * h, axis=-1, keepdims=True) + EPS) * g_ref[...]
    o_ref[...] = (hn * jax.nn.sigmoid(mo_ref[...].astype(F32))).astype(BF16)


def mlstm_output(h2, y, mlstm_norm, mo_col0, dv):
    t = y.shape[0]
    nh = MLSTM_HEADS
    tm = ROW_TILE
    return pl.pallas_call(
        _mlstm_out_kernel,
        grid=(t // tm, nh),
        in_specs=[
            pl.BlockSpec((2, tm, dv), lambda i, h: (0, i, h)),
            pl.BlockSpec((tm, dv), lambda i, h: (i, mo_col0 // dv + h)),
            pl.BlockSpec((1, dv), lambda i, h: (0, h)),
        ],
        out_specs=pl.BlockSpec((tm, dv), lambda i, h: (i, h)),
        out_shape=jax.ShapeDtypeStruct((t, nh * dv), BF16),
        compiler_params=_cparams(("arbitrary", "arbitrary"), 16 * tm * dv * 4),
        name="mlstm_out",
    )(h2, y, mlstm_norm.reshape(1, nh * dv))


def _out_proj_kernel(a_ref, m_ref, w1_ref, w2_ref, h_ref, mod_ref, o_ref, w1_sc, w2_sc, *, n_ctx, tm, gate_idx):
    @pl.when(pl.program_id(1) == 0)
    def _():
        w1_sc[...] = w1_ref[...].astype(BF16)
        w2_sc[...] = w2_ref[...].astype(BF16)

    acc = jnp.dot(a_ref[...], w1_sc[...], preferred_element_type=F32)
    acc = acc + jnp.dot(m_ref[...], w2_sc[...], preferred_element_type=F32)
    rows = pl.program_id(1) * tm + lax.broadcasted_iota(jnp.int32, (tm, 1), 0)
    o_ref[...] = h_ref[...] + jnp.tanh(_stream_vec(mod_ref, gate_idx, rows, n_ctx)) * acc


def out_projection(a, m, w_out, h, mod, n_ctx, gate_idx):
    t, half = a.shape
    d = w_out.shape[1]
    tm = _pick_tile(t, (768, 512, 256))
    tn = _pick_tile(d, (512, 256, 128))
    return pl.pallas_call(
        functools.partial(_out_proj_kernel, n_ctx=n_ctx, tm=tm, gate_idx=gate_idx),
        grid=(d // tn, t // tm),
        in_specs=[
            pl.BlockSpec((tm, half), lambda j, i: (i, 0)),
            pl.BlockSpec((tm, half), lambda j, i: (i, 0)),
            pl.BlockSpec((half, tn), lambda j, i: (0, j)),
            pl.BlockSpec((half, tn), lambda j, i: (1, j)),
            pl.BlockSpec((tm, tn), lambda j, i: (i, j)),
            pl.BlockSpec((2, 6, tn), lambda j, i: (0, 0, j)),
        ],
        out_specs=pl.BlockSpec((tm, tn), lambda j, i: (i, j)),
        out_shape=jax.ShapeDtypeStruct((t, d), F32),
        scratch_shapes=[pltpu.VMEM((half, tn), BF16), pltpu.VMEM((half, tn), BF16)],
        compiler_params=_cparams(("arbitrary", "arbitrary"),
                                 2 * (2 * tm * half * 2 + 2 * half * tn * 4 + 2 * tm * tn * 4) + 2 * half * tn * 2
                                 + tm * tn * 4),
        name="out_proj",
    )(a, m, w_out, w_out, h, mod)


def _pool_kernel(h_ref, hp_ref, hn_ref, g_ref, mod_ref, o_ref, z_sc, *, seq, tm):
    i = pl.program_id(0)
    nblk = pl.num_programs(0)
    halo = POOL_HALO
    d = h_ref.shape[1]
    gd = d // len(POOL_WINDOWS)

    def normed(x):
        y = x * lax.rsqrt(jnp.mean(x * x, axis=-1, keepdims=True) + EPS) * g_ref[...]
        return y * (1.0 + mod_ref[0, 1:2, :]) + mod_ref[0, 0:1, :]

    z_sc[0:halo, :] = jnp.where(i > 0, normed(hp_ref[...]), 0.0)
    z_sc[halo:halo + tm, :] = normed(h_ref[...])
    z_sc[halo + tm:2 * halo + tm, :] = jnp.where(i < nblk - 1, normed(hn_ref[...]), 0.0)

    t = i * tm + lax.broadcasted_iota(jnp.int32, (tm, 1), 0)
    for gi, w in enumerate(POOL_WINDOWS):
        left = w // 2
        right = w - 1 - left
        cols = slice(gi * gd, (gi + 1) * gd)
        acc = z_sc[halo - left:halo - left + tm, cols]
        for off in range(-left + 1, right + 1):
            acc = acc + z_sc[halo + off:halo + off + tm, cols]
        cnt = (jnp.minimum(t + right + 1, seq) - jnp.maximum(t - left, 0)).astype(F32)
        o_ref[:, cols] = (acc / cnt - z_sc[halo:halo + tm, cols]).astype(BF16)


def pool_features(h, g, mod):
    seq, d = h.shape
    tm = ROW_TILE
    halo = POOL_HALO
    r = tm // halo
    nhb = seq // halo
    return pl.pallas_call(
        functools.partial(_pool_kernel, seq=seq, tm=tm),
        grid=(seq // tm,),
        in_specs=[
            pl.BlockSpec((tm, d), lambda i: (i, 0)),
            pl.BlockSpec((halo, d), lambda i: (jnp.maximum(i * r - 1, 0), 0)),
            pl.BlockSpec((halo, d), lambda i: (jnp.minimum((i + 1) * r, nhb - 1), 0)),
            pl.BlockSpec((1, d), lambda i: (0, 0)),
            pl.BlockSpec((2, 6, d), lambda i: (0, 0, 0)),
        ],
        out_specs=pl.BlockSpec((tm, d), lambda i: (i, 0)),
        out_shape=jax.ShapeDtypeStruct((seq, d), BF16),
        scratch_shapes=[pltpu.VMEM((tm + 2 * halo, d), F32)],
        compiler_params=_cparams(("arbitrary",), 10 * tm * d * 4),
        name="pool_features",
    )(h, h, h, g.reshape(1, d), mod)


def _pool_proj_kernel(x_ref, w_ref, ps_ref, h_ref, mod_ref, o_ref):
    acc = jnp.dot(x_ref[...], w_ref[0], preferred_element_type=F32)
    o_ref[...] = h_ref[...] + jnp.tanh(mod_ref[0, 2:3, :]) * (acc * ps_ref[...])


def pool_projection(dp, pool_w, pool_scale, h, mod):
    seq, d = h.shape
    ng, gd, _ = pool_w.shape
    tm = _pick_tile(seq, (1024, 512, 256))
    return pl.pallas_call(
        _pool_proj_kernel,
        grid=(ng, seq // tm),
        in_specs=[
            pl.BlockSpec((tm, gd), lambda g, i: (i, g)),
            pl.BlockSpec((1, gd, gd), lambda g, i: (g, 0, 0)),
            pl.BlockSpec((1, gd), lambda g, i: (0, g)),
            pl.BlockSpec((tm, gd), lambda g, i: (i, g)),
            pl.BlockSpec((2, 6, gd), lambda g, i: (0, 0, g)),
        ],
        out_specs=pl.BlockSpec((tm, gd), lambda g, i: (i, g)),
        out_shape=jax.ShapeDtypeStruct((seq, d), F32),
        compiler_params=_cparams(("arbitrary", "arbitrary"), 2 * (tm * gd * 10 + gd * gd * 2) + tm * gd * 4),
        name="pool_proj",
    )(dp, pool_w, pool_scale.reshape(1, d), h, mod)


def _route_kernel(lg_ref, bias_ref, gate_ref, idx_ref, rank_ref, cnt_ref, run_sc, *, n_exp):
    ng = N_GROUPS
    ge = n_exp // ng
    tt = lg_ref.shape[1]
    neg = -jnp.inf

    @pl.when(pl.program_id(0) == 0)
    def _():
        run_sc[...] = jnp.zeros_like(run_sc)

    scores = jax.nn.sigmoid(lg_ref[...])
    sel = (scores + bias_ref[...]).reshape(ng, ge, tt)
    e_in_g = lax.broadcasted_iota(jnp.int32, (ng, ge, tt), 1)
    m1 = jnp.max(sel, axis=1, keepdims=True)
    first = jnp.min(jnp.where(sel == m1, e_in_g, ge), axis=1, keepdims=True)
    m2 = jnp.max(jnp.where(e_in_g == first, neg, sel), axis=1, keepdims=True)
    gscore = m1 + m2

    gid = lax.broadcasted_iota(jnp.int32, (ng, 1, tt), 0)
    gmask = jnp.zeros((ng, 1, tt), jnp.bool_)
    for _ in range(TOPK_GROUPS):
        mx = jnp.max(gscore, axis=0, keepdims=True)
        pick = gid == jnp.min(jnp.where(gscore == mx, gid, ng), axis=0, keepdims=True)
        gmask = jnp.logical_or(gmask, pick)
        gscore = jnp.where(pick, neg, gscore)

    eid = lax.broadcasted_iota(jnp.int32, (ng, ge, tt), 0) * ge + e_in_g
    cand = jnp.where(gmask, sel, neg)
    chosen = jnp.zeros((ng, ge, tt), jnp.bool_)
    for kk in range(TOP_K):
        mx = jnp.max(jnp.max(cand, axis=1, keepdims=True), axis=0, keepdims=True)
        hit = jnp.where(cand == mx, eid, n_exp)
        pick_id = jnp.min(jnp.min(hit, axis=1, keepdims=True), axis=0, keepdims=True)
        pick = eid == pick_id
        chosen = jnp.logical_or(chosen, pick)
        cand = jnp.where(pick, neg, cand)
        idx_ref[kk:kk + 1, :] = pick_id.reshape(1, tt)

    s3 = scores.reshape(ng, ge, tt)
    w = jnp.where(chosen, s3, 0.0)
    denom = jnp.sum(jnp.sum(w, axis=1, keepdims=True), axis=0, keepdims=True)
    gate_ref[...] = (w / denom * ROUTED_SCALE).reshape(n_exp, tt)

    chosen_f = chosen.reshape(n_exp, tt).astype(F32)
    earlier = (lax.broadcasted_iota(jnp.int32, (tt, tt), 0) < lax.broadcasted_iota(jnp.int32, (tt, tt), 1))
    local = jnp.dot(chosen_f.astype(BF16), earlier.astype(BF16), preferred_element_type=F32)
    run = run_sc[...]
    rank_ref[...] = jnp.where(chosen.reshape(n_exp, tt), local + run, -1.0).astype(jnp.int32)
    run = run + jnp.sum(chosen_f, axis=1, keepdims=True)
    run_sc[...] = run
    cnt_ref[...] = jnp.broadcast_to(run, cnt_ref.shape).astype(jnp.int32)


def route(logits_t, router_bias):
    n_exp, t = logits_t.shape
    tt = _pick_tile(t, (512, 256, 128))
    gates, idx, rank, cnt = pl.pallas_call(
        functools.partial(_route_kernel, n_exp=n_exp),
        grid=(t // tt,),
        in_specs=[pl.BlockSpec((n_exp, tt), lambda i: (0, i)), pl.BlockSpec((n_exp, 1), lambda i: (0, 0))],
        out_specs=[
            pl.BlockSpec((n_exp, tt), lambda i: (0, i)),
            pl.BlockSpec((TOP_K, tt), lambda i: (0, i)),
            pl.BlockSpec((n_exp, tt), lambda i: (0, i)),
            pl.BlockSpec((n_exp, LANES), lambda i: (0, 0)),
        ],
        out_shape=[
            jax.ShapeDtypeStruct((n_exp, t), F32),
            jax.ShapeDtypeStruct((TOP_K, t), jnp.int32),
            jax.ShapeDtypeStruct((n_exp, t), jnp.int32),
            jax.ShapeDtypeStruct((n_exp, LANES), jnp.int32),
        ],
        scratch_shapes=[pltpu.VMEM((n_exp, 1), F32)],
        compiler_params=_cparams(("arbitrary",), 32 * 2**20),
        name="moe_route",
    )(logits_t, router_bias.reshape(n_exp, 1))
    return gates, idx, rank, cnt[:, 0]


def _slab_copy(src, src_row, dst, dst_row, sem):
    return pltpu.make_async_copy(src.at[src_row], dst.at[dst_row], sem)


def _dispatch_kernel(pos_ref, zp_ref, xs_in_ref, xs_ref, sem, *, tt):
    del xs_in_ref

    def issue(n, carry):
        _slab_copy(zp_ref, n // TOP_K, xs_ref, pos_ref[0, 0, n], sem).start()
        return carry

    lax.fori_loop(0, tt * TOP_K, issue, 0)

    def drain(n, carry):
        _slab_copy(zp_ref, 0, xs_ref, 0, sem).wait()
        return carry

    lax.fori_loop(0, tt * TOP_K, drain, 0)


def dispatch(zp, pos_tk, n_rows):
    t, ns, _ = zp.shape
    tt = COMBINE_TILE
    xs0 = jnp.zeros((n_rows, ns, LANES), jnp.uint32)
    return pl.pallas_call(
        functools.partial(_dispatch_kernel, tt=tt),
        grid=(t // tt,),
        in_specs=[
            pl.BlockSpec((1, 1, tt * TOP_K), lambda i: (i, 0, 0), memory_space=pltpu.SMEM),
            pl.BlockSpec((tt, ns, LANES), lambda i: (i, 0, 0)),
            pl.BlockSpec(memory_space=pl.ANY),
        ],
        out_specs=pl.BlockSpec(memory_space=pl.ANY),
        out_shape=jax.ShapeDtypeStruct((n_rows, ns, LANES), jnp.uint32),
        scratch_shapes=[pltpu.SemaphoreType.DMA(())],
        input_output_aliases={2: 0},
        compiler_params=_cparams(("arbitrary",), 2 * tt * ns * LANES * 4),
        name="moe_dispatch",
    )(pos_tk.reshape(t // tt, 1, tt * TOP_K), zp, xs0)


def _expert_kernel(te_ref, first_ref, nused_ref, x_ref, wg_ref, wu_ref, wd_ref, y_ref, wg_sc, wu_sc, wd_sc):
    i = pl.program_id(0)

    @pl.when(i < nused_ref[0])
    def _():
        @pl.when(first_ref[i] == 1)
        def _():
            wg_sc[...] = wg_ref[0, 0].astype(BF16)
            wu_sc[...] = wu_ref[0, 0].astype(BF16)
            wd_sc[...] = wd_ref[0, 0].astype(BF16)

        x = _unpack_rows(_load_slabs(x_ref)).astype(BF16)
        g = jnp.dot(x, wg_sc[...], preferred_element_type=F32)
        u = jnp.dot(x, wu_sc[...], preferred_element_type=F32)
        hid = (g * jax.nn.sigmoid(g) * u).astype(BF16)
        y = jnp.dot(hid, wd_sc[...], preferred_element_type=F32)
        _store_slabs(y_ref, _pack_rows(y))

    @pl.when(i >= nused_ref[0])
    def _():
        y_ref[...] = jnp.zeros_like(y_ref)


def expert_ffn(xs, tile_expert, tile_first, n_used, layer, exp_gate, exp_up, exp_down):
    n_rows, ns, _ = xs.shape
    _, n_exp, d, f = exp_gate.shape
    tm = EXPERT_TILE
    n_tiles = n_rows // tm

    def row_blk(i, te, first, nused):
        return (jnp.minimum(i, nused[0] - 1), 0, 0)

    def w_blk(i, te, first, nused):
        return (layer, te[i], 0, 0)

    grid_spec = pltpu.PrefetchScalarGridSpec(
        num_scalar_prefetch=3,
        grid=(n_tiles,),
        in_specs=[
            pl.BlockSpec((tm, ns, LANES), row_blk),
            pl.BlockSpec((1, 1, d, f), w_blk),
            pl.BlockSpec((1, 1, d, f), w_blk),
            pl.BlockSpec((1, 1, f, d), w_blk),
        ],
        out_specs=pl.BlockSpec((tm, ns, LANES), lambda i, te, first, nused: (i, 0, 0)),
        scratch_shapes=[pltpu.VMEM((d, f), BF16), pltpu.VMEM((d, f), BF16), pltpu.VMEM((f, d), BF16)],
    )
    return pl.pallas_call(
        _expert_kernel,
        grid_spec=grid_spec,
        out_shape=jax.ShapeDtypeStruct((n_rows, ns, LANES), jnp.uint32),
        compiler_params=_cparams(("arbitrary",), 2 * 3 * d * f * 4 + 3 * d * f * 2 + 4 * tm * ns * LANES * 4
                                 + 6 * tm * d * 4),
        name="moe_experts",
    )(tile_expert, tile_first, n_used, xs, exp_gate, exp_up, exp_down)


def _combine_kernel(pos_ref, ys_ref, wt_ref, z_ref, sg_ref, su_ref, sd_ref, h_ref, mod_ref, o_ref, buf, sem,
                    *, tt, n_ctx, row0):
    def issue(n, carry):
        k = n % TOP_K
        _slab_copy(ys_ref, pos_ref[0, 0, n], buf.at[k], n // TOP_K, sem).start()
        return carry

    lax.fori_loop(0, tt * TOP_K, issue, 0)

    z = z_ref[...]
    g = jnp.dot(z, sg_ref[...], preferred_element_type=F32)
    u = jnp.dot(z, su_ref[...], preferred_element_type=F32)
    acc = jnp.dot((g * jax.nn.sigmoid(g) * u).astype(BF16), sd_ref[...], preferred_element_type=F32)

    def drain(n, carry):
        _slab_copy(ys_ref, 0, buf.at[0], 0, sem).wait()
        return carry

    lax.fori_loop(0, tt * TOP_K, drain, 0)

    wt = wt_ref[...]
    for k in range(TOP_K):
        acc = acc + wt[:, k:k + 1] * _unpack_rows(_load_slabs(buf.at[k]))
    rows = row0 + pl.program_id(0) * tt + lax.broadcasted_iota(jnp.int32, (tt, 1), 0)
    o_ref[...] = h_ref[...] + jnp.tanh(_stream_vec(mod_ref, 5, rows, n_ctx)) * acc


def combine(ys, pos_tk, w_tk, z, sh_gate, sh_up, sh_down, h, mod, n_ctx, row0):
    t, d = h.shape
    tt = COMBINE_TILE
    f = sh_gate.shape[1]
    ns = ys.shape[1]
    b0 = row0 // tt
    n_out = t - row0
    return pl.pallas_call(
        functools.partial(_combine_kernel, tt=tt, n_ctx=n_ctx, row0=row0),
        grid=(n_out // tt,),
        in_specs=[
            pl.BlockSpec((1, 1, tt * TOP_K), lambda i: (b0 + i, 0, 0), memory_space=pltpu.SMEM),
            pl.BlockSpec(memory_space=pl.ANY),
            pl.BlockSpec((tt, TOP_K), lambda i: (b0 + i, 0)),
            pl.BlockSpec((tt, d), lambda i: (b0 + i, 0)),
            pl.BlockSpec((d, f), lambda i: (0, 0)),
            pl.BlockSpec((d, f), lambda i: (0, 0)),
            pl.BlockSpec((f, d), lambda i: (0, 0)),
            pl.BlockSpec((tt, d), lambda i: (b0 + i, 0)),
            pl.BlockSpec((2, 6, d), lambda i: (0, 0, 0)),
        ],
        out_specs=pl.BlockSpec((tt, d), lambda i: (i, 0)),
        out_shape=jax.ShapeDtypeStruct((n_out, d), F32),
        scratch_shapes=[pltpu.VMEM((TOP_K, tt, ns, LANES), jnp.uint32), pltpu.SemaphoreType.DMA(())],
        compiler_params=_cparams(("arbitrary",), TOP_K * tt * ns * LANES * 4 + 12 * tt * d * 4 + 6 * d * f * 2 * 2),
        name="moe_combine",
    )(pos_tk.reshape(t // tt, 1, tt * TOP_K), ys, w_tk, z, sh_gate, sh_up, sh_down, h, mod)


def moe_block(h, norm_g, mod, n_ctx, row0, layer, router_w, router_bias, exp_gate, exp_up, exp_down,
              sh_gate, sh_up, sh_down):
    t, d = h.shape
    n_exp = router_w.shape[1]
    z, zp, logits = norm_modulate(h, norm_g, mod, n_ctx, 3, router_w=router_w)
    gates, idx_t, rank, counts = route(logits[:, :n_exp].T, router_bias)

    tm = EXPERT_TILE
    padded = (counts + tm - 1) // tm * tm
    ends = jnp.cumsum(padded)
    starts = ends - padded
    n_tiles = (t * TOP_K) // tm + n_exp
    tile_start = jnp.arange(n_tiles, dtype=jnp.int32) * tm
    n_used = (ends[-1] // tm).astype(jnp.int32)
    tile_expert = jnp.sum((ends[None, :] <= tile_start[:, None]).astype(jnp.int32), axis=1)
    tile_expert = jnp.minimum(tile_expert, n_exp - 1)
    tile_expert = jnp.where(tile_start < ends[-1], tile_expert, tile_expert[jnp.maximum(n_used - 1, 0)])
    tile_first = jnp.concatenate([jnp.ones((1,), jnp.int32), (tile_expert[1:] != tile_expert[:-1]).astype(jnp.int32)])

    pos_dense = starts[:, None] + rank
    pos_tk = jnp.take_along_axis(pos_dense, idx_t, axis=0).T.astype(jnp.int32)
    w_tk = jnp.take_along_axis(gates, idx_t, axis=0).T

    xs = dispatch(zp, pos_tk, n_tiles * tm)
    ys = expert_ffn(xs, tile_expert, tile_first, n_used.reshape(1), layer, exp_gate, exp_up, exp_down)
    return combine(ys, pos_tk, w_tk, z, sh_gate.astype(BF16), sh_up.astype(BF16), sh_down.astype(BF16),
                   h, mod, n_ctx, row0)


def mixer_layer(h, n_ctx, mod, norm_g, w_in, gate_bias, q_norm, k_norm, mlstm_norm, w_out, cos_t, sin_t):
    t, d = h.shape
    dh = ATTN_HEAD_DIM
    n_q = (d // 2) // dh
    n_kv = n_q // GQA_GROUP
    nh = MLSTM_HEADS
    dv = (d // 2) // nh
    dk = dv // 2
    sizes = (n_q * dh, n_kv * dh, n_kv * dh, nh * dk, nh * dk, nh * dv, nh * dv, 4 * nh)
    off = [0]
    for s in sizes:
        off.append(off[-1] + s)
    n_main = off[7]

    z = norm_modulate(h, norm_g, mod, n_ctx, 0)
    y = matmul(z, w_in, n_main, BF16)
    wg = jnp.zeros((d, LANES), F32).at[:, :4 * nh].set(w_in[:, n_main:])
    gates_raw = matmul(z, wg, LANES, F32)

    qt, k, vt = qk_prepare(y, cos_t, sin_t, q_norm, k_norm, n_q, n_kv)
    a = attention(qt, k, vt, n_ctx)
    h2 = mlstm(y, gates_raw, gate_bias, off[3], off[4], off[5], dk, dv)
    m = mlstm_output(h2, y, mlstm_norm, off[6], dv)
    return out_projection(a, m, w_out, h, mod, n_ctx, 2)


def kernel(x, c, ctx, c_ctx, ada_w, ada_b, norm_mix, norm_ffn, w_in, gate_bias, q_norm, k_norm, mlstm_norm, w_out,
           pool_w, pool_scale, router_w, router_bias, exp_gate, exp_up, exp_down, sh_gate, sh_up, sh_down):
    depth = ada_w.shape[0]
    seq = x.shape[1]
    n_ctx = ctx.shape[1]
    assert x.shape[0] == 1 and n_ctx == ROW_TILE and seq % ROW_TILE == 0
    mods = ada_modulation(c, c_ctx, ada_w, ada_b)
    cos_t, sin_t = rope_tables(n_ctx, seq)

    h = jnp.concatenate([ctx[0], x[0]], axis=0)
    nc = n_ctx
    for layer in range(depth):
        last = layer == depth - 1
        j = layer // 2
        mod = mods[layer]
        if layer % 2 == 0:
            h = mixer_layer(h, nc, mod, norm_mix[layer], w_in[j], gate_bias[j], q_norm[j], k_norm[j],
                            mlstm_norm[j], w_out[j], cos_t, sin_t)
        else:
            assert nc == 0 or not last
            dp = pool_features(h, norm_mix[layer], mod)
            h = pool_projection(dp, pool_w[j].astype(BF16), pool_scale[j], h, mod)
        drop_ctx = nc > 0 and (last or (layer + 1 == depth - 1 and (depth - 1) % 2 == 1))
        row0 = nc if drop_ctx else 0
        h = moe_block(h, norm_ffn[layer], mod, nc, row0, layer, router_w[layer], router_bias[layer], exp_gate,
                      exp_up, exp_down, sh_gate[layer], sh_up[layer], sh_down[layer])
        if drop_ctx:
            nc = 0
    return h[nc:][None]
```

```python
import functools

import jax
import jax.numpy as jnp
from jax import lax
from jax.experimental import pallas as pl
from jax.experimental.pallas import tpu as pltpu

F32 = jnp.float32
BF16 = jnp.bfloat16

EPS = 1e-6
ATTN_HEAD_DIM = 128
GQA_GROUP = 4
ROPE_THETA = 10000.0
GRID_W = 64
MLSTM_HEADS = 4
GATE_SOFTCAP = 15.0
POOL_WINDOWS = (2, 4, 8, 16)
POOL_HALO = 16
TOP_K = 8
N_GROUPS = 8
TOPK_GROUPS = 4
ROUTED_SCALE = 2.5

LANES = 128
V7X_VMEM_BUDGET = 56 * 2**20
COMPILER_TEMP_BYTES = 8 * 2**20
ROW_TILE = 256
MLSTM_CHUNK = 256
ATTN_KV_CHUNK = 512
EXPERT_TILE = 256
COMBINE_TILE = 128


def _cparams(sem, vmem_bytes):
    limit = min(vmem_bytes + COMPILER_TEMP_BYTES, V7X_VMEM_BUDGET)
    return pltpu.CompilerParams(dimension_semantics=sem, vmem_limit_bytes=int(limit))


def _pick_tile(n, candidates):
    for c in candidates:
        if n % c == 0:
            return c
    raise ValueError(f"no tile for {n}")


ADA_K_CHUNK = 512


def _ada_kernel(s_ref, w_ref, b_ref, o_ref):
    d = w_ref.shape[1]
    tn = w_ref.shape[2]
    sub = 8

    def body(kc, acc):
        k0 = pl.multiple_of(kc * ADA_K_CHUNK, ADA_K_CHUNK)
        s = s_ref[pl.ds(k0, ADA_K_CHUNK), :]
        s = (s * jax.nn.sigmoid(s)).reshape(ADA_K_CHUNK // sub, sub, LANES)
        w = w_ref[0, pl.ds(k0, ADA_K_CHUNK), :].reshape(ADA_K_CHUNK // sub, sub, tn)
        return tuple(a + jnp.sum(w * s[:, :, j:j + 1], axis=0) for j, a in enumerate(acc))

    acc = lax.fori_loop(0, d // ADA_K_CHUNK, body, (jnp.zeros((sub, tn), F32), jnp.zeros((sub, tn), F32)))
    for j, a in enumerate(acc):
        o_ref[0, j:j + 1, :] = jnp.sum(a, axis=0, keepdims=True) + b_ref[0]


def ada_modulation(c, c_ctx, ada_w, ada_b):
    depth, d, n = ada_w.shape
    s = jnp.zeros((d, LANES), F32).at[:, 0].set(c[0]).at[:, 1].set(c_ctx)
    tn = _pick_tile(n, (1024, 512, 256, 128))
    out = pl.pallas_call(
        _ada_kernel,
        grid=(depth, n // tn),
        in_specs=[
            pl.BlockSpec((d, LANES), lambda l, j: (0, 0)),
            pl.BlockSpec((1, d, tn), lambda l, j: (l, 0, j)),
            pl.BlockSpec((1, 1, tn), lambda l, j: (l, 0, j)),
        ],
        out_specs=pl.BlockSpec((1, 2, tn), lambda l, j: (l, 0, j)),
        out_shape=jax.ShapeDtypeStruct((depth, 2, n), F32),
        compiler_params=_cparams(("arbitrary", "arbitrary"), 2 * d * tn * 4 + 8 * ADA_K_CHUNK * tn * 4),
        name="ada_mod",
    )(s, ada_w, ada_b.reshape(depth, 1, n))
    return out.reshape(depth, 2, 6, d)


def _stream_vec(mod_ref, idx, rows, n_ctx):
    lat = mod_ref[0, idx:idx + 1, :]
    if n_ctx == 0:
        return lat
    return jnp.where(rows < n_ctx, mod_ref[1, idx:idx + 1, :], lat)


def _pack_rows(x):
    half = x.shape[1] // 2
    return pltpu.pack_elementwise([x[:, :half], x[:, half:]], packed_dtype=BF16)


def _unpack_rows(xp):
    lo = pltpu.unpack_elementwise(xp, index=0, packed_dtype=BF16, unpacked_dtype=F32)
    hi = pltpu.unpack_elementwise(xp, index=1, packed_dtype=BF16, unpacked_dtype=F32)
    return jnp.concatenate([lo, hi], axis=1)


SUBLANES = 8


def _sublane_transpose(v):
    ax = v[0].ndim - 2
    sub = lax.broadcasted_iota(jnp.int32, v[0].shape, ax)
    for b in (4, 2, 1):
        upper = (sub & b) != 0
        nxt = []
        for a in range(SUBLANES):
            other = v[a ^ b]
            if a & b == 0:
                nxt.append(jnp.where(upper, pltpu.roll(other, b, axis=ax), v[a]))
            else:
                nxt.append(jnp.where(upper, v[a], pltpu.roll(other, SUBLANES - b, axis=ax)))
        v = nxt
    return v


def _store_slabs(ref, xp):
    rows, ns, _ = ref.shape
    x = pltpu.bitcast(xp, jnp.int32)
    u = [x[:, c * LANES:(c + 1) * LANES].reshape(rows // SUBLANES, SUBLANES, LANES) for c in range(ns)]
    v = [jnp.stack([u[SUBLANES * h + s] for h in range(ns // SUBLANES)], axis=1) for s in range(SUBLANES)]
    w = _sublane_transpose(v)
    ref[...] = pltpu.bitcast(jnp.stack(w, axis=1).reshape(rows, ns, LANES), jnp.uint32)


def _load_slabs(ref):
    rows, ns, _ = ref.shape
    x = pltpu.bitcast(ref[...], jnp.int32).reshape(rows // SUBLANES, SUBLANES, ns // SUBLANES, SUBLANES, LANES)
    v = _sublane_transpose([x[:, j] for j in range(SUBLANES)])
    cols = [v[s][:, h].reshape(rows, LANES) for h in range(ns // SUBLANES) for s in range(SUBLANES)]
    return pltpu.bitcast(jnp.concatenate(cols, axis=1), jnp.uint32)


def _norm_mod_kernel(h_ref, g_ref, mod_ref, *rest, n_ctx, tm, shift_idx, with_router):
    if with_router:
        rw_ref, z_ref, zp_ref, lg_ref = rest
    else:
        (z_ref,) = rest
    x = h_ref[...]
    rows = pl.program_id(0) * tm + lax.broadcasted_iota(jnp.int32, (tm, 1), 0)
    ms = jnp.mean(x * x, axis=-1, keepdims=True)
    y = x * lax.rsqrt(ms + EPS) * g_ref[...]
    z = y * (1.0 + _stream_vec(mod_ref, shift_idx + 1, rows, n_ctx)) + _stream_vec(mod_ref, shift_idx, rows, n_ctx)
    z_ref[...] = z.astype(BF16)
    if with_router:
        _store_slabs(zp_ref, _pack_rows(z))
        lg_ref[...] = jnp.dot(z, rw_ref[...], preferred_element_type=F32, precision=lax.Precision.HIGHEST)


def norm_modulate(h, g, mod, n_ctx, shift_idx, router_w=None):
    t, d = h.shape
    tm = ROW_TILE
    with_router = router_w is not None
    in_specs = [
        pl.BlockSpec((tm, d), lambda i: (i, 0)),
        pl.BlockSpec((1, d), lambda i: (0, 0)),
        pl.BlockSpec((2, 6, d), lambda i: (0, 0, 0)),
    ]
    args = [h, g.reshape(1, d), mod]
    out_specs = [pl.BlockSpec((tm, d), lambda i: (i, 0))]
    out_shape = [jax.ShapeDtypeStruct((t, d), BF16)]
    if with_router:
        e = router_w.shape[1]
        rw = jnp.zeros((d, LANES), F32).at[:, :e].set(router_w)
        in_specs.append(pl.BlockSpec((d, LANES), lambda i: (0, 0)))
        args.append(rw)
        ns = d // 2 // LANES
        out_specs += [pl.BlockSpec((tm, ns, LANES), lambda i: (i, 0, 0)), pl.BlockSpec((tm, LANES), lambda i: (i, 0))]
        out_shape += [jax.ShapeDtypeStruct((t, ns, LANES), jnp.uint32), jax.ShapeDtypeStruct((t, LANES), F32)]
    outs = pl.pallas_call(
        functools.partial(_norm_mod_kernel, n_ctx=n_ctx, tm=tm, shift_idx=shift_idx, with_router=with_router),
        grid=(t // tm,),
        in_specs=in_specs,
        out_specs=out_specs,
        out_shape=out_shape,
        compiler_params=_cparams(("arbitrary",), 8 * tm * d * 4 + 4 * d * LANES * 4),
        name="norm_mod_router" if with_router else "norm_mod",
    )(*args)
    return outs if with_router else outs[0]


def _mm_kernel(x_ref, w_ref, o_ref, wb_sc):
    @pl.when(pl.program_id(1) == 0)
    def _():
        wb_sc[...] = w_ref[...].astype(BF16)

    o_ref[...] = jnp.dot(x_ref[...], wb_sc[...], preferred_element_type=F32).astype(o_ref.dtype)


def matmul(x, w, n, out_dtype):
    m, k = x.shape
    tm = _pick_tile(m, (1056, 1024, 768, 640, 512, 256))
    tn = _pick_tile(n, (512, 256, 128))
    osz = jnp.dtype(out_dtype).itemsize
    wsz = jnp.dtype(w.dtype).itemsize
    return pl.pallas_call(
        _mm_kernel,
        grid=(n // tn, m // tm),
        in_specs=[pl.BlockSpec((tm, k), lambda j, i: (i, 0)), pl.BlockSpec((k, tn), lambda j, i: (0, j))],
        out_specs=pl.BlockSpec((tm, tn), lambda j, i: (i, j)),
        out_shape=jax.ShapeDtypeStruct((m, n), out_dtype),
        scratch_shapes=[pltpu.VMEM((k, tn), BF16)],
        compiler_params=_cparams(("arbitrary", "arbitrary"),
                                 2 * (tm * k * 2 + k * tn * wsz + tm * tn * osz) + k * tn * 2 + tm * tn * 4),
        name="matmul",
    )(x, w)


LOG2_E = 1.4426950408889634


def _qk_prep_kernel(y_ref, cos_ref, sin_ref, qg_ref, kg_ref, qt_ref, k_ref, vt_ref, *, n_q, n_k):
    dh = ATTN_HEAD_DIM
    tq = y_ref.shape[0]
    cos = cos_ref[...]
    sin = sin_ref[...]
    lane = lax.broadcasted_iota(jnp.int32, cos.shape, 1)
    first_half = (lane % (dh // 2)) < (dh // 4)
    q_scale = dh ** -0.5 * LOG2_E
    for hd in range(n_q + n_k):
        x = y_ref[:, hd * dh:(hd + 1) * dh].astype(F32)
        g = qg_ref[...] if hd < n_q else kg_ref[...]
        y = x * lax.rsqrt(jnp.mean(x * x, axis=-1, keepdims=True) + EPS) * g
        partner = jnp.where(first_half, pltpu.roll(y, dh - dh // 4, axis=1), pltpu.roll(y, dh // 4, axis=1))
        r = y * cos + partner * sin
        if hd < n_q:
            kv, g_in = divmod(hd, GQA_GROUP)
            qt_ref[kv, 0, :, g_in * tq:(g_in + 1) * tq] = (r * q_scale).T.astype(BF16)
        else:
            k_ref[:, (hd - n_q) * dh:(hd - n_q + 1) * dh] = r.astype(BF16)
    for kv in range(n_k):
        v = y_ref[:, (n_q + n_k + kv) * dh:(n_q + n_k + kv + 1) * dh].astype(F32)
        vt_ref[kv, 0] = v.T.astype(BF16)


def qk_prepare(y, cos_t, sin_t, q_norm, k_norm, n_q, n_k):
    t = y.shape[0]
    dh = ATTN_HEAD_DIM
    w = (n_q + 2 * n_k) * dh
    tm = ROW_TILE
    return pl.pallas_call(
        functools.partial(_qk_prep_kernel, n_q=n_q, n_k=n_k),
        grid=(t // tm,),
        in_specs=[
            pl.BlockSpec((tm, w), lambda i: (i, 0)),
            pl.BlockSpec((tm, dh), lambda i: (i, 0)),
            pl.BlockSpec((tm, dh), lambda i: (i, 0)),
            pl.BlockSpec((1, dh), lambda i: (0, 0)),
            pl.BlockSpec((1, dh), lambda i: (0, 0)),
        ],
        out_specs=[
            pl.BlockSpec((n_k, 1, dh, GQA_GROUP * tm), lambda i: (0, i, 0, 0)),
            pl.BlockSpec((tm, n_k * dh), lambda i: (i, 0)),
            pl.BlockSpec((n_k, 1, dh, tm), lambda i: (0, i, 0, 0)),
        ],
        out_shape=[
            jax.ShapeDtypeStruct((n_k, t // tm, dh, GQA_GROUP * tm), BF16),
            jax.ShapeDtypeStruct((t, n_k * dh), BF16),
            jax.ShapeDtypeStruct((n_k, t // tm, dh, tm), BF16),
        ],
        compiler_params=_cparams(("arbitrary",), 10 * tm * w * 4),
        name="qk_prep",
    )(y, cos_t, sin_t, q_norm.reshape(1, dh), k_norm.reshape(1, dh))


def rope_tables(n_ctx, seq):
    nf = ATTN_HEAD_DIM // 4
    rows = seq // GRID_W
    row = jnp.repeat(jnp.arange(rows, dtype=F32), GRID_W)
    col = jnp.tile(jnp.arange(GRID_W, dtype=F32), rows)
    inv_freq = ROPE_THETA ** (-jnp.arange(nf, dtype=F32) / nf)
    ang = jnp.stack([row, col], axis=-1)[..., None] * inv_freq
    cos, sin = jnp.cos(ang), jnp.sin(ang)
    cos_l = jnp.concatenate([cos, cos], axis=-1).reshape(seq, 4 * nf)
    sin_l = jnp.concatenate([-sin, sin], axis=-1).reshape(seq, 4 * nf)
    cos_t = jnp.concatenate([jnp.ones((n_ctx, 4 * nf), F32), cos_l], axis=0)
    sin_t = jnp.concatenate([jnp.zeros((n_ctx, 4 * nf), F32), sin_l], axis=0)
    return cos_t, sin_t


def _attn_kernel(qt_ref, k_ref, vt_ref, o_ref, m_sc, l_sc, acc_sc, sa_sc, sb_sc, *, n_ctx, kc, n_lat_chunks):
    dh = ATTN_HEAD_DIM
    tq = o_ref.shape[0]
    qt = qt_ref[0, 0]

    s = jnp.dot(k_ref[0:n_ctx, :], qt, preferred_element_type=F32)
    m0 = jnp.max(s, axis=0, keepdims=True)
    p = jnp.exp2(s - m0)
    m_sc[...] = m0
    l_sc[...] = jnp.sum(p, axis=0, keepdims=True)
    acc_sc[...] = jnp.dot(vt_ref[0, 0], p.astype(BF16), preferred_element_type=F32)
    tiles = kc // n_ctx

    def scores(c):
        start = pl.multiple_of(n_ctx + jnp.minimum(c, n_lat_chunks - 1) * kc, n_ctx)
        return jnp.dot(k_ref[pl.ds(start, kc), :], qt, preferred_element_type=F32)

    def update(s, c):
        m_prev = m_sc[...]
        m_new = jnp.maximum(m_prev, jnp.max(s, axis=0, keepdims=True))
        alpha = jnp.exp2(m_prev - m_new)
        p = jnp.exp2(s - m_new)
        l_sc[...] = alpha * l_sc[...] + jnp.sum(p, axis=0, keepdims=True)
        vt = jnp.concatenate([vt_ref[0, 1 + c * tiles + j] for j in range(tiles)], axis=1)
        pv = jnp.dot(vt, p.astype(BF16), preferred_element_type=F32)
        acc_sc[...] = alpha * acc_sc[...] + pv
        m_sc[...] = m_new

    sa_sc[...] = scores(0)

    def body(i, carry):
        c = 2 * i
        sb_sc[...] = scores(c + 1)
        update(sa_sc[...], c)
        sa_sc[...] = scores(c + 2)
        update(sb_sc[...], c + 1)
        return carry

    lax.fori_loop(0, jnp.where(pl.program_id(1) == 0, 0, n_lat_chunks // 2), body, 0)
    out = acc_sc[...] / l_sc[...]
    for g in range(GQA_GROUP):
        o_ref[:, g * dh:(g + 1) * dh] = out[:, g * tq:(g + 1) * tq].T.astype(BF16)


def attention(qt, k, vt, n_ctx):
    n_kv, nblk, dh, cols = qt.shape
    t = k.shape[0]
    tq = cols // GQA_GROUP
    assert tq == n_ctx
    kc = _pick_tile(t - n_ctx, (2 * ATTN_KV_CHUNK, 2 * n_ctx)) // 2
    return pl.pallas_call(
        functools.partial(_attn_kernel, n_ctx=n_ctx, kc=kc, n_lat_chunks=(t - n_ctx) // kc),
        grid=(n_kv, nblk),
        in_specs=[
            pl.BlockSpec((1, 1, dh, cols), lambda h, i: (h, i, 0, 0)),
            pl.BlockSpec((t, dh), lambda h, i: (0, h)),
            pl.BlockSpec((1, nblk, dh, tq), lambda h, i: (h, 0, 0, 0)),
        ],
        out_specs=pl.BlockSpec((tq, GQA_GROUP * dh), lambda h, i: (i, h)),
        out_shape=jax.ShapeDtypeStruct((t, n_kv * GQA_GROUP * dh), BF16),
        scratch_shapes=[pltpu.VMEM((1, cols), F32), pltpu.VMEM((1, cols), F32), pltpu.VMEM((dh, cols), F32),
                        pltpu.VMEM((kc, cols), F32), pltpu.VMEM((kc, cols), F32)],
        compiler_params=_cparams(("arbitrary", "arbitrary"), 4 * t * dh * 2 + 8 * kc * cols * 4 + 4 * dh * cols * 4),
        name="attention",
    )(qt, k, vt)


def _mlstm_kernel(q_ref, k_ref, v_ref, g_ref, b_ref, h_ref, ct_sc, m_sc, *, n_heads, dk, dv):
    ln = q_ref.shape[0]
    dh_id = pl.program_id(0)
    direction = dh_id // n_heads
    head = dh_id % n_heads
    col_i = direction * 2 * n_heads + head
    col_f = col_i + n_heads

    @pl.when(pl.program_id(1) == 0)
    def _():
        ct_sc[...] = jnp.zeros_like(ct_sc)
        m_sc[...] = jnp.zeros_like(m_sc)

    pre = g_ref[...] + b_ref[...]
    pre = GATE_SOFTCAP * jnp.tanh(pre / GATE_SOFTCAP)
    lane = lax.broadcasted_iota(jnp.int32, pre.shape, 1)
    is_forget = ((lane // n_heads) % 2) == 1
    gates = jnp.where(is_forget, jax.nn.log_sigmoid(pre), pre)

    r = lax.broadcasted_iota(jnp.int32, (ln, ln), 0)
    c = lax.broadcasted_iota(jnp.int32, (ln, ln), 1)
    allowed = jnp.where(direction == 0, r - c, c - r) >= 0
    cum = jnp.dot(allowed.astype(F32), gates, preferred_element_type=F32, precision=lax.Precision.HIGHEST)

    def pick_col(a, idx):
        return jnp.sum(jnp.where(lane == idx, a, 0.0), axis=1, keepdims=True)

    sub = lax.broadcasted_iota(jnp.int32, (LANES, ln), 0)

    def pick_row(a, idx):
        return jnp.sum(jnp.where(sub == idx, a.T, 0.0), axis=0, keepdims=True)

    b_col = pick_col(cum, col_f)
    i_col = pick_col(gates, col_i)
    b_row = pick_row(cum, col_f)
    i_row = pick_row(gates, col_i)
    m_prev = m_sc[...]

    log_d = jnp.where(allowed, b_col - b_row + i_row, -jnp.inf)
    log_inter = b_col + m_prev
    m_t = jnp.maximum(log_inter, jnp.max(log_d, axis=1, keepdims=True))
    scale = dk ** -0.5
    d_mat = jnp.exp(log_d - m_t) * scale
    w_inter = jnp.exp(log_inter - m_t) * scale

    q = q_ref[...]
    k = k_ref[...]
    v_aug = jnp.concatenate([v_ref[...], jnp.ones((ln, LANES), BF16)], axis=1)
    s = lax.dot_general(q, k, (((1,), (1,)), ((), ())), preferred_element_type=F32) * d_mat
    intra = jnp.dot(s.astype(BF16), v_aug, preferred_element_type=F32)
    ct = ct_sc[...]
    inter = jnp.dot(q, ct.astype(BF16), preferred_element_type=F32)
    nd = intra + w_inter * inter
    den = nd[:, dv:dv + 1]
    h_ref[0] = nd[:, :dv] / jnp.maximum(jnp.abs(den), jnp.exp(-m_t))

    g_tot = jnp.sum(pick_col(gates, col_f), axis=0, keepdims=True)
    log_w = g_tot - b_col + i_col
    m_new = jnp.maximum(g_tot + m_prev, jnp.max(log_w, axis=0, keepdims=True))
    w = jnp.exp(log_w - m_new)
    decay = jnp.exp(g_tot + m_prev - m_new)
    wv = (w * v_aug.astype(F32)).astype(BF16)
    upd = lax.dot_general(k, wv, (((0,), (0,)), ((), ())), preferred_element_type=F32)
    ct_sc[...] = decay * ct + upd
    m_sc[...] = m_new


def mlstm(y, gates_raw, gate_bias, q_col0, k_col0, v_col0, dk, dv):
    t = y.shape[0]
    nh = MLSTM_HEADS
    ln = MLSTM_CHUNK
    nblk = t // ln
    bias = jnp.zeros((1, LANES), F32).at[0, :4 * nh].set(gate_bias)

    def blk(dhid, j):
        return jnp.where(dhid // nh == 0, j, jnp.where(j == 0, 0, nblk - j))

    return pl.pallas_call(
        functools.partial(_mlstm_kernel, n_heads=nh, dk=dk, dv=dv),
        grid=(2 * nh, nblk),
        in_specs=[
            pl.BlockSpec((ln, dk), lambda d, j: (blk(d, j), q_col0 // dk + d % nh)),
            pl.BlockSpec((ln, dk), lambda d, j: (blk(d, j), k_col0 // dk + d % nh)),
            pl.BlockSpec((ln, dv), lambda d, j: (blk(d, j), v_col0 // dv + d % nh)),
            pl.BlockSpec((ln, LANES), lambda d, j: (blk(d, j), 0)),
            pl.BlockSpec((1, LANES), lambda d, j: (0, 0)),
        ],
        out_specs=pl.BlockSpec((1, ln, dv), lambda d, j: (d // nh, blk(d, j), d % nh)),
        out_shape=jax.ShapeDtypeStruct((2, t, nh * dv), F32),
        scratch_shapes=[pltpu.VMEM((dk, dv + LANES), F32), pltpu.VMEM((1, 1), F32)],
        compiler_params=_cparams(("arbitrary", "arbitrary"), 32 * 2**20),
        name="mlstm_scan",
    )(y, y, y, gates_raw, bias)


def _mlstm_out_kernel(h_ref, mo_ref, g_ref, o_ref):
    h = h_ref[0] + h_ref[1]
    hn = h * lax.rsqrt(jnp.mean(h * h, axis=-1, keepdims=True) + EPS) * g_ref[...]
    o_ref[...] = (hn * jax.nn.sigmoid(mo_ref[...].astype(F32))).astype(BF16)


def mlstm_output(h2, y, mlstm_norm, mo_col0, dv):
    t = y.shape[0]
    nh = MLSTM_HEADS
    tm = ROW_TILE
    return pl.pallas_call(
        _mlstm_out_kernel,
        grid=(t // tm, nh),
        in_specs=[
            pl.BlockSpec((2, tm, dv), lambda i, h: (0, i, h)),
            pl.BlockSpec((tm, dv), lambda i, h: (i, mo_col0 // dv + h)),
            pl.BlockSpec((1, dv), lambda i, h: (0, h)),
        ],
        out_specs=pl.BlockSpec((tm, dv), lambda i, h: (i, h)),
        out_shape=jax.ShapeDtypeStruct((t, nh * dv), BF16),
        compiler_params=_cparams(("arbitrary", "arbitrary"), 16 * tm * dv * 4),
        name="mlstm_out",
    )(h2, y, mlstm_norm.reshape(1, nh * dv))


def _out_proj_kernel(a_ref, m_ref, w1_ref, w2_ref, h_ref, mod_ref, o_ref, w1_sc, w2_sc, *, n_ctx, tm, gate_idx):
    @pl.when(pl.program_id(1) == 0)
    def _():
        w1_sc[...] = w1_ref[...].astype(BF16)
        w2_sc[...] = w2_ref[...].astype(BF16)

    acc = jnp.dot(a_ref[...], w1_sc[...], preferred_element_type=F32)
    acc = acc + jnp.dot(m_ref[...], w2_sc[...], preferred_element_type=F32)
    rows = pl.program_id(1) * tm + lax.broadcasted_iota(jnp.int32, (tm, 1), 0)
    o_ref[...] = h_ref[...] + jnp.tanh(_stream_vec(mod_ref, gate_idx, rows, n_ctx)) * acc


def out_projection(a, m, w_out, h, mod, n_ctx, gate_idx):
    t, half = a.shape
    d = w_out.shape[1]
    tm = _pick_tile(t, (768, 512, 256))
    tn = _pick_tile(d, (512, 256, 128))
    return pl.pallas_call(
        functools.partial(_out_proj_kernel, n_ctx=n_ctx, tm=tm, gate_idx=gate_idx),
        grid=(d // tn, t // tm),
        in_specs=[
            pl.BlockSpec((tm, half), lambda j, i: (i, 0)),
            pl.BlockSpec((tm, half), lambda j, i: (i, 0)),
            pl.BlockSpec((half, tn), lambda j, i: (0, j)),
            pl.BlockSpec((half, tn), lambda j, i: (1, j)),
            pl.BlockSpec((tm, tn), lambda j, i: (i, j)),
            pl.BlockSpec((2, 6, tn), lambda j, i: (0, 0, j)),
        ],
        out_specs=pl.BlockSpec((tm, tn), lambda j, i: (i, j)),
        out_shape=jax.ShapeDtypeStruct((t, d), F32),
        scratch_shapes=[pltpu.VMEM((half, tn), BF16), pltpu.VMEM((half, tn), BF16)],
        compiler_params=_cparams(("arbitrary", "arbitrary"),
                                 2 * (2 * tm * half * 2 + 2 * half * tn * 4 + 2 * tm * tn * 4) + 2 * half * tn * 2
                                 + tm * tn * 4),
        name="out_proj",
    )(a, m, w_out, w_out, h, mod)


def _pool_kernel(h_ref, hp_ref, hn_ref, g_ref, mod_ref, o_ref, z_sc, *, seq, tm):
    i = pl.program_id(0)
    nblk = pl.num_programs(0)
    halo = POOL_HALO
    d = h_ref.shape[1]
    gd = d // len(POOL_WINDOWS)

    def normed(x):
        y = x * lax.rsqrt(jnp.mean(x * x, axis=-1, keepdims=True) + EPS) * g_ref[...]
        return y * (1.0 + mod_ref[0, 1:2, :]) + mod_ref[0, 0:1, :]

    z_sc[0:halo, :] = jnp.where(i > 0, normed(hp_ref[...]), 0.0)
    z_sc[halo:halo + tm, :] = normed(h_ref[...])
    z_sc[halo + tm:2 * halo + tm, :] = jnp.where(i < nblk - 1, normed(hn_ref[...]), 0.0)

    t = i * tm + lax.broadcasted_iota(jnp.int32, (tm, 1), 0)
    for gi, w in enumerate(POOL_WINDOWS):
        left = w // 2
        right = w - 1 - left
        cols = slice(gi * gd, (gi + 1) * gd)
        acc = z_sc[halo - left:halo - left + tm, cols]
        for off in range(-left + 1, right + 1):
            acc = acc + z_sc[halo + off:halo + off + tm, cols]
        cnt = (jnp.minimum(t + right + 1, seq) - jnp.maximum(t - left, 0)).astype(F32)
        o_ref[:, cols] = (acc / cnt - z_sc[halo:halo + tm, cols]).astype(BF16)


def pool_features(h, g, mod):
    seq, d = h.shape
    tm = ROW_TILE
    halo = POOL_HALO
    r = tm // halo
    nhb = seq // halo
    return pl.pallas_call(
        functools.partial(_pool_kernel, seq=seq, tm=tm),
        grid=(seq // tm,),
        in_specs=[
            pl.BlockSpec((tm, d), lambda i: (i, 0)),
            pl.BlockSpec((halo, d), lambda i: (jnp.maximum(i * r - 1, 0), 0)),
            pl.BlockSpec((halo, d), lambda i: (jnp.minimum((i + 1) * r, nhb - 1), 0)),
            pl.BlockSpec((1, d), lambda i: (0, 0)),
            pl.BlockSpec((2, 6, d), lambda i: (0, 0, 0)),
        ],
        out_specs=pl.BlockSpec((tm, d), lambda i: (i, 0)),
        out_shape=jax.ShapeDtypeStruct((seq, d), BF16),
        scratch_shapes=[pltpu.VMEM((tm + 2 * halo, d), F32)],
        compiler_params=_cparams(("arbitrary",), 10 * tm * d * 4),
        name="pool_features",
    )(h, h, h, g.reshape(1, d), mod)


def _pool_proj_kernel(x_ref, w_ref, ps_ref, h_ref, mod_ref, o_ref):
    acc = jnp.dot(x_ref[...], w_ref[0], preferred_element_type=F32)
    o_ref[...] = h_ref[...] + jnp.tanh(mod_ref[0, 2:3, :]) * (acc * ps_ref[...])


def pool_projection(dp, pool_w, pool_scale, h, mod):
    seq, d = h.shape
    ng, gd, _ = pool_w.shape
    tm = _pick_tile(seq, (1024, 512, 256))
    return pl.pallas_call(
        _pool_proj_kernel,
        grid=(ng, seq // tm),
        in_specs=[
            pl.BlockSpec((tm, gd), lambda g, i: (i, g)),
            pl.BlockSpec((1, gd, gd), lambda g, i: (g, 0, 0)),
            pl.BlockSpec((1, gd), lambda g, i: (0, g)),
            pl.BlockSpec((tm, gd), lambda g, i: (i, g)),
            pl.BlockSpec((2, 6, gd), lambda g, i: (0, 0, g)),
        ],
        out_specs=pl.BlockSpec((tm, gd), lambda g, i: (i, g)),
        out_shape=jax.ShapeDtypeStruct((seq, d), F32),
        compiler_params=_cparams(("arbitrary", "arbitrary"), 2 * (tm * gd * 10 + gd * gd * 2) + tm * gd * 4),
        name="pool_proj",
    )(dp, pool_w, pool_scale.reshape(1, d), h, mod)


def _route_kernel(lg_ref, bias_ref, gate_ref, idx_ref, rank_ref, cnt_ref, run_sc, *, n_exp):
    ng = N_GROUPS
    ge = n_exp // ng
    tt = lg_ref.shape[1]
    neg = -jnp.inf

    @pl.when(pl.program_id(0) == 0)
    def _():
        run_sc[...] = jnp.zeros_like(run_sc)

    scores = jax.nn.sigmoid(lg_ref[...])
    sel = (scores + bias_ref[...]).reshape(ng, ge, tt)
    e_in_g = lax.broadcasted_iota(jnp.int32, (ng, ge, tt), 1)
    m1 = jnp.max(sel, axis=1, keepdims=True)
    first = jnp.min(jnp.where(sel == m1, e_in_g, ge), axis=1, keepdims=True)
    m2 = jnp.max(jnp.where(e_in_g == first, neg, sel), axis=1, keepdims=True)
    gscore = m1 + m2

    gid = lax.broadcasted_iota(jnp.int32, (ng, 1, tt), 0)
    gmask = jnp.zeros((ng, 1, tt), jnp.bool_)
    for _ in range(TOPK_GROUPS):
        mx = jnp.max(gscore, axis=0, keepdims=True)
        pick = gid == jnp.min(jnp.where(gscore == mx, gid, ng), axis=0, keepdims=True)
        gmask = jnp.logical_or(gmask, pick)
        gscore = jnp.where(pick, neg, gscore)

    eid = lax.broadcasted_iota(jnp.int32, (ng, ge, tt), 0) * ge + e_in_g
    cand = jnp.where(gmask, sel, neg)
    chosen = jnp.zeros((ng, ge, tt), jnp.bool_)
    for kk in range(TOP_K):
        mx = jnp.max(jnp.max(cand, axis=1, keepdims=True), axis=0, keepdims=True)
        hit = jnp.where(cand == mx, eid, n_exp)
        pick_id = jnp.min(jnp.min(hit, axis=1, keepdims=True), axis=0, keepdims=True)
        pick = eid == pick_id
        chosen = jnp.logical_or(chosen, pick)
        cand = jnp.where(pick, neg, cand)
        idx_ref[kk:kk + 1, :] = pick_id.reshape(1, tt)

    s3 = scores.reshape(ng, ge, tt)
    w = jnp.where(chosen, s3, 0.0)
    denom = jnp.sum(jnp.sum(w, axis=1, keepdims=True), axis=0, keepdims=True)
    gate_ref[...] = (w / denom * ROUTED_SCALE).reshape(n_exp, tt)

    chosen_f = chosen.reshape(n_exp, tt).astype(F32)
    earlier = (lax.broadcasted_iota(jnp.int32, (tt, tt), 0) < lax.broadcasted_iota(jnp.int32, (tt, tt), 1))
    local = jnp.dot(chosen_f.astype(BF16), earlier.astype(BF16), preferred_element_type=F32)
    run = run_sc[...]
    rank_ref[...] = jnp.where(chosen.reshape(n_exp, tt), local + run, -1.0).astype(jnp.int32)
    run = run + jnp.sum(chosen_f, axis=1, keepdims=True)
    run_sc[...] = run
    cnt_ref[...] = jnp.broadcast_to(run, cnt_ref.shape).astype(jnp.int32)


def route(logits_t, router_bias):
    n_exp, t = logits_t.shape
    tt = _pick_tile(t, (512, 256, 128))
    gates, idx, rank, cnt = pl.pallas_call(
        functools.partial(_route_kernel, n_exp=n_exp),
        grid=(t // tt,),
        in_specs=[pl.BlockSpec((n_exp, tt), lambda i: (0, i)), pl.BlockSpec((n_exp, 1), lambda i: (0, 0))],
        out_specs=[
            pl.BlockSpec((n_exp, tt), lambda i: (0, i)),
            pl.BlockSpec((TOP_K, tt), lambda i: (0, i)),
            pl.BlockSpec((n_exp, tt), lambda i: (0, i)),
            pl.BlockSpec((n_exp, LANES), lambda i: (0, 0)),
        ],
        out_shape=[
            jax.ShapeDtypeStruct((n_exp, t), F32),
            jax.ShapeDtypeStruct((TOP_K, t), jnp.int32),
            jax.ShapeDtypeStruct((n_exp, t), jnp.int32),
            jax.ShapeDtypeStruct((n_exp, LANES), jnp.int32),
        ],
        scratch_shapes=[pltpu.VMEM((n_exp, 1), F32)],
        compiler_params=_cparams(("arbitrary",), 32 * 2**20),
        name="moe_route",
    )(logits_t, router_bias.reshape(n_exp, 1))
    return gates, idx, rank, cnt[:, 0]


def _slab_copy(src, src_row, dst, dst_row, sem):
    return pltpu.make_async_copy(src.at[src_row], dst.at[dst_row], sem)


def _dispatch_kernel(pos_ref, zp_ref, xs_in_ref, xs_ref, sem, *, tt):
    del xs_in_ref

    def issue(j, carry):
        for k in range(TOP_K):
            _slab_copy(zp_ref, j, xs_ref, pos_ref[0, 0, j * TOP_K + k], sem).start()
        return carry

    lax.fori_loop(0, tt, issue, 0)
    for _ in range(tt * TOP_K):
        _slab_copy(zp_ref, 0, xs_ref, 0, sem).wait()


def dispatch(zp, pos_tk, n_rows):
    t, ns, _ = zp.shape
    tt = COMBINE_TILE
    xs0 = jnp.zeros((n_rows, ns, LANES), jnp.uint32)
    return pl.pallas_call(
        functools.partial(_dispatch_kernel, tt=tt),
        grid=(t // tt,),
        in_specs=[
            pl.BlockSpec((1, 1, tt * TOP_K), lambda i: (i, 0, 0), memory_space=pltpu.SMEM),
            pl.BlockSpec((tt, ns, LANES), lambda i: (i, 0, 0)),
            pl.BlockSpec(memory_space=pl.ANY),
        ],
        out_specs=pl.BlockSpec(memory_space=pl.ANY),
        out_shape=jax.ShapeDtypeStruct((n_rows, ns, LANES), jnp.uint32),
        scratch_shapes=[pltpu.SemaphoreType.DMA(())],
        input_output_aliases={2: 0},
        compiler_params=_cparams(("arbitrary",), 2 * tt * ns * LANES * 4),
        name="moe_dispatch",
    )(pos_tk.reshape(t // tt, 1, tt * TOP_K), zp, xs0)


def _expert_kernel(te_ref, first_ref, nused_ref, x_ref, wg_ref, wu_ref, wd_ref, y_ref, wg_sc, wu_sc, wd_sc):
    i = pl.program_id(0)

    @pl.when(i < nused_ref[0])
    def _():
        @pl.when(first_ref[i] == 1)
        def _():
            wg_sc[...] = wg_ref[0, 0].astype(BF16)
            wu_sc[...] = wu_ref[0, 0].astype(BF16)
            wd_sc[...] = wd_ref[0, 0].astype(BF16)

        x = _unpack_rows(_load_slabs(x_ref)).astype(BF16)
        g = jnp.dot(x, wg_sc[...], preferred_element_type=F32)
        u = jnp.dot(x, wu_sc[...], preferred_element_type=F32)
        hid = (g * jax.nn.sigmoid(g) * u).astype(BF16)
        y = jnp.dot(hid, wd_sc[...], preferred_element_type=F32)
        _store_slabs(y_ref, _pack_rows(y))

    @pl.when(i >= nused_ref[0])
    def _():
        y_ref[...] = jnp.zeros_like(y_ref)


def expert_ffn(xs, tile_expert, tile_first, n_used, layer, exp_gate, exp_up, exp_down):
    n_rows, ns, _ = xs.shape
    _, n_exp, d, f = exp_gate.shape
    tm = EXPERT_TILE
    n_tiles = n_rows // tm

    def row_blk(i, te, first, nused):
        return (jnp.minimum(i, nused[0] - 1), 0, 0)

    def w_blk(i, te, first, nused):
        return (layer, te[i], 0, 0)

    grid_spec = pltpu.PrefetchScalarGridSpec(
        num_scalar_prefetch=3,
        grid=(n_tiles,),
        in_specs=[
            pl.BlockSpec((tm, ns, LANES), row_blk),
            pl.BlockSpec((1, 1, d, f), w_blk),
            pl.BlockSpec((1, 1, d, f), w_blk),
            pl.BlockSpec((1, 1, f, d), w_blk),
        ],
        out_specs=pl.BlockSpec((tm, ns, LANES), lambda i, te, first, nused: (i, 0, 0)),
        scratch_shapes=[pltpu.VMEM((d, f), BF16), pltpu.VMEM((d, f), BF16), pltpu.VMEM((f, d), BF16)],
    )
    return pl.pallas_call(
        _expert_kernel,
        grid_spec=grid_spec,
        out_shape=jax.ShapeDtypeStruct((n_rows, ns, LANES), jnp.uint32),
        compiler_params=_cparams(("arbitrary",), 2 * 3 * d * f * 4 + 3 * d * f * 2 + 4 * tm * ns * LANES * 4
                                 + 6 * tm * d * 4),
        name="moe_experts",
    )(tile_expert, tile_first, n_used, xs, exp_gate, exp_up, exp_down)


def _combine_kernel(pos_ref, ys_ref, wt_ref, z_ref, sg_ref, su_ref, sd_ref, h_ref, mod_ref, o_ref, buf, sem,
                    *, tt, n_ctx, row0):
    def issue(j, carry):
        for k in range(TOP_K):
            _slab_copy(ys_ref, pos_ref[0, 0, j * TOP_K + k], buf.at[k], j, sem).start()
        return carry

    lax.fori_loop(0, tt, issue, 0)

    z = z_ref[...]
    g = jnp.dot(z, sg_ref[...], preferred_element_type=F32)
    u = jnp.dot(z, su_ref[...], preferred_element_type=F32)
    acc = jnp.dot((g * jax.nn.sigmoid(g) * u).astype(BF16), sd_ref[...], preferred_element_type=F32)

    for _ in range(tt * TOP_K):
        _slab_copy(ys_ref, 0, buf.at[0], 0, sem).wait()

    wt = wt_ref[...]
    for k in range(TOP_K):
        acc = acc + wt[:, k:k + 1] * _unpack_rows(_load_slabs(buf.at[k]))
    rows = row0 + pl.program_id(0) * tt + lax.broadcasted_iota(jnp.int32, (tt, 1), 0)
    o_ref[...] = h_ref[...] + jnp.tanh(_stream_vec(mod_ref, 5, rows, n_ctx)) * acc


def combine(ys, pos_tk, w_tk, z, sh_gate, sh_up, sh_down, h, mod, n_ctx, row0):
    t, d = h.shape
    tt = COMBINE_TILE
    f = sh_gate.shape[1]
    ns = ys.shape[1]
    b0 = row0 // tt
    n_out = t - row0
    return pl.pallas_call(
        functools.partial(_combine_kernel, tt=tt, n_ctx=n_ctx, row0=row0),
        grid=(n_out // tt,),
        in_specs=[
            pl.BlockSpec((1, 1, tt * TOP_K), lambda i: (b0 + i, 0, 0), memory_space=pltpu.SMEM),
            pl.BlockSpec(memory_space=pl.ANY),
            pl.BlockSpec((tt, TOP_K), lambda i: (b0 + i, 0)),
            pl.BlockSpec((tt, d), lambda i: (b0 + i, 0)),
            pl.BlockSpec((d, f), lambda i: (0, 0)),
            pl.BlockSpec((d, f), lambda i: (0, 0)),
            pl.BlockSpec((f, d), lambda i: (0, 0)),
            pl.BlockSpec((tt, d), lambda i: (b0 + i, 0)),
            pl.BlockSpec((2, 6, d), lambda i: (0, 0, 0)),
        ],
        out_specs=pl.BlockSpec((tt, d), lambda i: (i, 0)),
        out_shape=jax.ShapeDtypeStruct((n_out, d), F32),
        scratch_shapes=[pltpu.VMEM((TOP_K, tt, ns, LANES), jnp.uint32), pltpu.SemaphoreType.DMA(())],
        compiler_params=_cparams(("arbitrary",), TOP_K * tt * ns * LANES * 4 + 12 * tt * d * 4 + 6 * d * f * 2 * 2),
        name="moe_combine",
    )(pos_tk.reshape(t // tt, 1, tt * TOP_K), ys, w_tk, z, sh_gate, sh_up, sh_down, h, mod)


def moe_block(h, norm_g, mod, n_ctx, row0, layer, router_w, router_bias, exp_gate, exp_up, exp_down,
              sh_gate, sh_up, sh_down):
    t, d = h.shape
    n_exp = router_w.shape[1]
    z, zp, logits = norm_modulate(h, norm_g, mod, n_ctx, 3, router_w=router_w)
    gates, idx_t, rank, counts = route(logits[:, :n_exp].T, router_bias)

    tm = EXPERT_TILE
    padded = (counts + tm - 1) // tm * tm
    ends = jnp.cumsum(padded)
    starts = ends - padded
    n_tiles = (t * TOP_K) // tm + n_exp
    tile_start = jnp.arange(n_tiles, dtype=jnp.int32) * tm
    n_used = (ends[-1] // tm).astype(jnp.int32)
    tile_expert = jnp.sum((ends[None, :] <= tile_start[:, None]).astype(jnp.int32), axis=1)
    tile_expert = jnp.minimum(tile_expert, n_exp - 1)
    tile_expert = jnp.where(tile_start < ends[-1], tile_expert, tile_expert[jnp.maximum(n_used - 1, 0)])
    tile_first = jnp.concatenate([jnp.ones((1,), jnp.int32), (tile_expert[1:] != tile_expert[:-1]).astype(jnp.int32)])

    pos_dense = starts[:, None] + rank
    pos_tk = jnp.take_along_axis(pos_dense, idx_t, axis=0).T.astype(jnp.int32)
    w_tk = jnp.take_along_axis(gates, idx_t, axis=0).T

    xs = dispatch(zp, pos_tk, n_tiles * tm)
    ys = expert_ffn(xs, tile_expert, tile_first, n_used.reshape(1), layer, exp_gate, exp_up, exp_down)
    return combine(ys, pos_tk, w_tk, z, sh_gate.astype(BF16), sh_up.astype(BF16), sh_down.astype(BF16),
                   h, mod, n_ctx, row0)


def mixer_layer(h, n_ctx, mod, norm_g, w_in, gate_bias, q_norm, k_norm, mlstm_norm, w_out, cos_t, sin_t):
    t, d = h.shape
    dh = ATTN_HEAD_DIM
    n_q = (d // 2) // dh
    n_kv = n_q // GQA_GROUP
    nh = MLSTM_HEADS
    dv = (d // 2) // nh
    dk = dv // 2
    sizes = (n_q * dh, n_kv * dh, n_kv * dh, nh * dk, nh * dk, nh * dv, nh * dv, 4 * nh)
    off = [0]
    for s in sizes:
        off.append(off[-1] + s)
    n_main = off[7]

    z = norm_modulate(h, norm_g, mod, n_ctx, 0)
    y = matmul(z, w_in, n_main, BF16)
    wg = jnp.zeros((d, LANES), F32).at[:, :4 * nh].set(w_in[:, n_main:])
    gates_raw = matmul(z, wg, LANES, F32)

    qt, k, vt = qk_prepare(y, cos_t, sin_t, q_norm, k_norm, n_q, n_kv)
    a = attention(qt, k, vt, n_ctx)
    h2 = mlstm(y, gates_raw, gate_bias, off[3], off[4], off[5], dk, dv)
    m = mlstm_output(h2, y, mlstm_norm, off[6], dv)
    return out_projection(a, m, w_out, h, mod, n_ctx, 2)


def kernel(x, c, ctx, c_ctx, ada_w, ada_b, norm_mix, norm_ffn, w_in, gate_bias, q_norm, k_norm, mlstm_norm, w_out,
           pool_w, pool_scale, router_w, router_bias, exp_gate, exp_up, exp_down, sh_gate, sh_up, sh_down):
    depth = ada_w.shape[0]
    seq = x.shape[1]
    n_ctx = ctx.shape[1]
    assert x.shape[0] == 1 and n_ctx == ROW_TILE and seq % ROW_TILE == 0
    mods = ada_modulation(c, c_ctx, ada_w, ada_b)
    cos_t, sin_t = rope_tables(n_ctx, seq)

    h = jnp.concatenate([ctx[0], x[0]], axis=0)
    nc = n_ctx
    for layer in range(depth):
        last = layer == depth - 1
        j = layer // 2
        mod = mods[layer]
        if layer % 2 == 0:
            h = mixer_layer(h, nc, mod, norm_mix[layer], w_in[j], gate_bias[j], q_norm[j], k_norm[j],
                            mlstm_norm[j], w_out[j], cos_t, sin_t)
        else:
            assert nc == 0 or not last
            dp = pool_features(h, norm_mix[layer], mod)
            h = pool_projection(dp, pool_w[j].astype(BF16), pool_scale[j], h, mod)
        drop_ctx = nc > 0 and (last or (layer + 1 == depth - 1 and (depth - 1) % 2 == 1))
        row0 = nc if drop_ctx else 0
        h = moe_block(h, norm_ffn[layer], mod, nc, row0, layer, router_w[layer], router_bias[layer], exp_gate,
                      exp_up, exp_down, sh_gate[layer], sh_up[layer], sh_down[layer])
        if drop_ctx:
            nc = 0
    return h[nc:][None]
```

```python
import functools

import jax
import jax.numpy as jnp
from jax import lax
from jax.experimental import pallas as pl
from jax.experimental.pallas import tpu as pltpu

F32 = jnp.float32
BF16 = jnp.bfloat16

EPS = 1e-6
ATTN_HEAD_DIM = 128
GQA_GROUP = 4
ROPE_THETA = 10000.0
GRID_W = 64
MLSTM_HEADS = 4
GATE_SOFTCAP = 15.0
POOL_WINDOWS = (2, 4, 8, 16)
POOL_HALO = 16
TOP_K = 8
N_GROUPS = 8
TOPK_GROUPS = 4
ROUTED_SCALE = 2.5

LANES = 128
V7X_VMEM_BUDGET = 56 * 2**20
COMPILER_TEMP_BYTES = 8 * 2**20
ROW_TILE = 256
MLSTM_CHUNK = 256
ATTN_KV_CHUNK = 512
EXPERT_TILE = 256
COMBINE_TILE = 128


def _cparams(sem, vmem_bytes):
    limit = min(vmem_bytes + COMPILER_TEMP_BYTES, V7X_VMEM_BUDGET)
    return pltpu.CompilerParams(dimension_semantics=sem, vmem_limit_bytes=int(limit))


def _pick_tile(n, candidates):
    for c in candidates:
        if n % c == 0:
            return c
    raise ValueError(f"no tile for {n}")


ADA_K_CHUNK = 512


def _ada_kernel(s_ref, w_ref, b_ref, o_ref):
    d = w_ref.shape[1]
    tn = w_ref.shape[2]
    sub = 8

    def body(kc, acc):
        k0 = pl.multiple_of(kc * ADA_K_CHUNK, ADA_K_CHUNK)
        s = s_ref[pl.ds(k0, ADA_K_CHUNK), :]
        s = (s * jax.nn.sigmoid(s)).reshape(ADA_K_CHUNK // sub, sub, LANES)
        w = w_ref[0, pl.ds(k0, ADA_K_CHUNK), :].reshape(ADA_K_CHUNK // sub, sub, tn)
        return tuple(a + jnp.sum(w * s[:, :, j:j + 1], axis=0) for j, a in enumerate(acc))

    acc = lax.fori_loop(0, d // ADA_K_CHUNK, body, (jnp.zeros((sub, tn), F32), jnp.zeros((sub, tn), F32)))
    for j, a in enumerate(acc):
        o_ref[0, j:j + 1, :] = jnp.sum(a, axis=0, keepdims=True) + b_ref[0]


def ada_modulation(c, c_ctx, ada_w, ada_b):
    depth, d, n = ada_w.shape
    s = jnp.zeros((d, LANES), F32).at[:, 0].set(c[0]).at[:, 1].set(c_ctx)
    tn = _pick_tile(n, (1024, 512, 256, 128))
    out = pl.pallas_call(
        _ada_kernel,
        grid=(depth, n // tn),
        in_specs=[
            pl.BlockSpec((d, LANES), lambda l, j: (0, 0)),
            pl.BlockSpec((1, d, tn), lambda l, j: (l, 0, j)),
            pl.BlockSpec((1, 1, tn), lambda l, j: (l, 0, j)),
        ],
        out_specs=pl.BlockSpec((1, 2, tn), lambda l, j: (l, 0, j)),
        out_shape=jax.ShapeDtypeStruct((depth, 2, n), F32),
        compiler_params=_cparams(("arbitrary", "arbitrary"), 2 * d * tn * 4 + 8 * ADA_K_CHUNK * tn * 4),
        name="ada_mod",
    )(s, ada_w, ada_b.reshape(depth, 1, n))
    return out.reshape(depth, 2, 6, d)


def _stream_vec(mod_ref, idx, rows, n_ctx):
    lat = mod_ref[0, idx:idx + 1, :]
    if n_ctx == 0:
        return lat
    return jnp.where(rows < n_ctx, mod_ref[1, idx:idx + 1, :], lat)


def _pack_rows(x):
    half = x.shape[1] // 2
    return pltpu.pack_elementwise([x[:, :half], x[:, half:]], packed_dtype=BF16)


def _unpack_rows(xp):
    lo = pltpu.unpack_elementwise(xp, index=0, packed_dtype=BF16, unpacked_dtype=F32)
    hi = pltpu.unpack_elementwise(xp, index=1, packed_dtype=BF16, unpacked_dtype=F32)
    return jnp.concatenate([lo, hi], axis=1)


SUBLANES = 8


def _sublane_transpose(v):
    ax = v[0].ndim - 2
    sub = lax.broadcasted_iota(jnp.int32, v[0].shape, ax)
    for b in (4, 2, 1):
        upper = (sub & b) != 0
        nxt = []
        for a in range(SUBLANES):
            other = v[a ^ b]
            if a & b == 0:
                nxt.append(jnp.where(upper, pltpu.roll(other, b, axis=ax), v[a]))
            else:
                nxt.append(jnp.where(upper, v[a], pltpu.roll(other, SUBLANES - b, axis=ax)))
        v = nxt
    return v


def _store_slabs(ref, xp):
    rows, ns, _ = ref.shape
    x = pltpu.bitcast(xp, jnp.int32)
    u = [x[:, c * LANES:(c + 1) * LANES].reshape(rows // SUBLANES, SUBLANES, LANES) for c in range(ns)]
    v = [jnp.stack([u[SUBLANES * h + s] for h in range(ns // SUBLANES)], axis=1) for s in range(SUBLANES)]
    w = _sublane_transpose(v)
    ref[...] = pltpu.bitcast(jnp.stack(w, axis=1).reshape(rows, ns, LANES), jnp.uint32)


def _load_slabs(ref):
    rows, ns, _ = ref.shape
    x = pltpu.bitcast(ref[...], jnp.int32).reshape(rows // SUBLANES, SUBLANES, ns // SUBLANES, SUBLANES, LANES)
    v = _sublane_transpose([x[:, j] for j in range(SUBLANES)])
    cols = [v[s][:, h].reshape(rows, LANES) for h in range(ns // SUBLANES) for s in range(SUBLANES)]
    return pltpu.bitcast(jnp.concatenate(cols, axis=1), jnp.uint32)


def _norm_mod_kernel(h_ref, g_ref, mod_ref, *rest, n_ctx, tm, shift_idx, with_router):
    if with_router:
        rw_ref, z_ref, zp_ref, lg_ref = rest
    else:
        (z_ref,) = rest
    x = h_ref[...]
    rows = pl.program_id(0) * tm + lax.broadcasted_iota(jnp.int32, (tm, 1), 0)
    ms = jnp.mean(x * x, axis=-1, keepdims=True)
    y = x * lax.rsqrt(ms + EPS) * g_ref[...]
    z = y * (1.0 + _stream_vec(mod_ref, shift_idx + 1, rows, n_ctx)) + _stream_vec(mod_ref, shift_idx, rows, n_ctx)
    z_ref[...] = z.astype(BF16)
    if with_router:
        _store_slabs(zp_ref, _pack_rows(z))
        lg_ref[...] = jnp.dot(z, rw_ref[...], preferred_element_type=F32, precision=lax.Precision.HIGHEST)


def norm_modulate(h, g, mod, n_ctx, shift_idx, router_w=None):
    t, d = h.shape
    tm = ROW_TILE
    with_router = router_w is not None
    in_specs = [
        pl.BlockSpec((tm, d), lambda i: (i, 0)),
        pl.BlockSpec((1, d), lambda i: (0, 0)),
        pl.BlockSpec((2, 6, d), lambda i: (0, 0, 0)),
    ]
    args = [h, g.reshape(1, d), mod]
    out_specs = [pl.BlockSpec((tm, d), lambda i: (i, 0))]
    out_shape = [jax.ShapeDtypeStruct((t, d), BF16)]
    if with_router:
        e = router_w.shape[1]
        rw = jnp.zeros((d, LANES), F32).at[:, :e].set(router_w)
        in_specs.append(pl.BlockSpec((d, LANES), lambda i: (0, 0)))
        args.append(rw)
        ns = d // 2 // LANES
        out_specs += [pl.BlockSpec((tm, ns, LANES), lambda i: (i, 0, 0)), pl.BlockSpec((tm, LANES), lambda i: (i, 0))]
        out_shape += [jax.ShapeDtypeStruct((t, ns, LANES), jnp.uint32), jax.ShapeDtypeStruct((t, LANES), F32)]
    outs = pl.pallas_call(
        functools.partial(_norm_mod_kernel, n_ctx=n_ctx, tm=tm, shift_idx=shift_idx, with_router=with_router),
        grid=(t // tm,),
        in_specs=in_specs,
        out_specs=out_specs,
        out_shape=out_shape,
        compiler_params=_cparams(("arbitrary",), 8 * tm * d * 4 + 4 * d * LANES * 4),
        name="norm_mod_router" if with_router else "norm_mod",
    )(*args)
    return outs if with_router else outs[0]


def _mm_kernel(x_ref, w_ref, o_ref, wb_sc):
    @pl.when(pl.program_id(1) == 0)
    def _():
        wb_sc[...] = w_ref[...].astype(BF16)

    o_ref[...] = jnp.dot(x_ref[...], wb_sc[...], preferred_element_type=F32).astype(o_ref.dtype)


def matmul(x, w, n, out_dtype):
    m, k = x.shape
    tm = _pick_tile(m, (1056, 1024, 768, 640, 512, 256))
    tn = _pick_tile(n, (512, 256, 128))
    osz = jnp.dtype(out_dtype).itemsize
    wsz = jnp.dtype(w.dtype).itemsize
    return pl.pallas_call(
        _mm_kernel,
        grid=(n // tn, m // tm),
        in_specs=[pl.BlockSpec((tm, k), lambda j, i: (i, 0)), pl.BlockSpec((k, tn), lambda j, i: (0, j))],
        out_specs=pl.BlockSpec((tm, tn), lambda j, i: (i, j)),
        out_shape=jax.ShapeDtypeStruct((m, n), out_dtype),
        scratch_shapes=[pltpu.VMEM((k, tn), BF16)],
        compiler_params=_cparams(("arbitrary", "arbitrary"),
                                 2 * (tm * k * 2 + k * tn * wsz + tm * tn * osz) + k * tn * 2 + tm * tn * 4),
        name="matmul",
    )(x, w)


LOG2_E = 1.4426950408889634


def _qk_prep_kernel(y_ref, cos_ref, sin_ref, qg_ref, kg_ref, qt_ref, k_ref, vt_ref, *, n_q, n_k):
    dh = ATTN_HEAD_DIM
    tq = y_ref.shape[0]
    cos = cos_ref[...]
    sin = sin_ref[...]
    lane = lax.broadcasted_iota(jnp.int32, cos.shape, 1)
    first_half = (lane % (dh // 2)) < (dh // 4)
    q_scale = dh ** -0.5 * LOG2_E
    for hd in range(n_q + n_k):
        x = y_ref[:, hd * dh:(hd + 1) * dh].astype(F32)
        g = qg_ref[...] if hd < n_q else kg_ref[...]
        y = x * lax.rsqrt(jnp.mean(x * x, axis=-1, keepdims=True) + EPS) * g
        partner = jnp.where(first_half, pltpu.roll(y, dh - dh // 4, axis=1), pltpu.roll(y, dh // 4, axis=1))
        r = y * cos + partner * sin
        if hd < n_q:
            kv, g_in = divmod(hd, GQA_GROUP)
            qt_ref[kv, 0, :, g_in * tq:(g_in + 1) * tq] = (r * q_scale).T.astype(BF16)
        else:
            k_ref[:, (hd - n_q) * dh:(hd - n_q + 1) * dh] = r.astype(BF16)
    for kv in range(n_k):
        v = y_ref[:, (n_q + n_k + kv) * dh:(n_q + n_k + kv + 1) * dh].astype(F32)
        vt_ref[kv, 0] = v.T.astype(BF16)


def qk_prepare(y, cos_t, sin_t, q_norm, k_norm, n_q, n_k):
    t = y.shape[0]
    dh = ATTN_HEAD_DIM
    w = (n_q + 2 * n_k) * dh
    tm = ROW_TILE
    return pl.pallas_call(
        functools.partial(_qk_prep_kernel, n_q=n_q, n_k=n_k),
        grid=(t // tm,),
        in_specs=[
            pl.BlockSpec((tm, w), lambda i: (i, 0)),
            pl.BlockSpec((tm, dh), lambda i: (i, 0)),
            pl.BlockSpec((tm, dh), lambda i: (i, 0)),
            pl.BlockSpec((1, dh), lambda i: (0, 0)),
            pl.BlockSpec((1, dh), lambda i: (0, 0)),
        ],
        out_specs=[
            pl.BlockSpec((n_k, 1, dh, GQA_GROUP * tm), lambda i: (0, i, 0, 0)),
            pl.BlockSpec((tm, n_k * dh), lambda i: (i, 0)),
            pl.BlockSpec((n_k, 1, dh, tm), lambda i: (0, i, 0, 0)),
        ],
        out_shape=[
            jax.ShapeDtypeStruct((n_k, t // tm, dh, GQA_GROUP * tm), BF16),
            jax.ShapeDtypeStruct((t, n_k * dh), BF16),
            jax.ShapeDtypeStruct((n_k, t // tm, dh, tm), BF16),
        ],
        compiler_params=_cparams(("arbitrary",), 10 * tm * w * 4),
        name="qk_prep",
    )(y, cos_t, sin_t, q_norm.reshape(1, dh), k_norm.reshape(1, dh))


def rope_tables(n_ctx, seq):
    nf = ATTN_HEAD_DIM // 4
    rows = seq // GRID_W
    row = jnp.repeat(jnp.arange(rows, dtype=F32), GRID_W)
    col = jnp.tile(jnp.arange(GRID_W, dtype=F32), rows)
    inv_freq = ROPE_THETA ** (-jnp.arange(nf, dtype=F32) / nf)
    ang = jnp.stack([row, col], axis=-1)[..., None] * inv_freq
    cos, sin = jnp.cos(ang), jnp.sin(ang)
    cos_l = jnp.concatenate([cos, cos], axis=-1).reshape(seq, 4 * nf)
    sin_l = jnp.concatenate([-sin, sin], axis=-1).reshape(seq, 4 * nf)
    cos_t = jnp.concatenate([jnp.ones((n_ctx, 4 * nf), F32), cos_l], axis=0)
    sin_t = jnp.concatenate([jnp.zeros((n_ctx, 4 * nf), F32), sin_l], axis=0)
    return cos_t, sin_t


def _attn_kernel(qt_ref, k_ref, vt_ref, o_ref, m_sc, l_sc, acc_sc, sa_sc, sb_sc, *, n_ctx, kc, n_lat_chunks):
    dh = ATTN_HEAD_DIM
    tq = o_ref.shape[0]
    qt = qt_ref[0, 0]

    s = jnp.dot(k_ref[0:n_ctx, :], qt, preferred_element_type=F32)
    m0 = jnp.max(s, axis=0, keepdims=True)
    p = jnp.exp2(s - m0)
    m_sc[...] = m0
    l_sc[...] = jnp.sum(p, axis=0, keepdims=True)
    acc_sc[...] = jnp.dot(vt_ref[0, 0], p.astype(BF16), preferred_element_type=F32)
    tiles = kc // n_ctx

    def scores(c):
        start = pl.multiple_of(n_ctx + jnp.minimum(c, n_lat_chunks - 1) * kc, n_ctx)
        return jnp.dot(k_ref[pl.ds(start, kc), :], qt, preferred_element_type=F32)

    def update(s, c):
        m_prev = m_sc[...]
        m_new = jnp.maximum(m_prev, jnp.max(s, axis=0, keepdims=True))
        alpha = jnp.exp2(m_prev - m_new)
        p = jnp.exp2(s - m_new)
        l_sc[...] = alpha * l_sc[...] + jnp.sum(p, axis=0, keepdims=True)
        vt = jnp.concatenate([vt_ref[0, 1 + c * tiles + j] for j in range(tiles)], axis=1)
        pv = jnp.dot(vt, p.astype(BF16), preferred_element_type=F32)
        acc_sc[...] = alpha * acc_sc[...] + pv
        m_sc[...] = m_new

    sa_sc[...] = scores(0)

    def body(i, carry):
        c = 2 * i
        sb_sc[...] = scores(c + 1)
        update(sa_sc[...], c)
        sa_sc[...] = scores(c + 2)
        update(sb_sc[...], c + 1)
        return carry

    lax.fori_loop(0, jnp.where(pl.program_id(1) == 0, 0, n_lat_chunks // 2), body, 0)
    out = acc_sc[...] / l_sc[...]
    for g in range(GQA_GROUP):
        o_ref[:, g * dh:(g + 1) * dh] = out[:, g * tq:(g + 1) * tq].T.astype(BF16)


def attention(qt, k, vt, n_ctx):
    n_kv, nblk, dh, cols = qt.shape
    t = k.shape[0]
    tq = cols // GQA_GROUP
    assert tq == n_ctx
    kc = _pick_tile(t - n_ctx, (2 * ATTN_KV_CHUNK, 2 * n_ctx)) // 2
    return pl.pallas_call(
        functools.partial(_attn_kernel, n_ctx=n_ctx, kc=kc, n_lat_chunks=(t - n_ctx) // kc),
        grid=(n_kv, nblk),
        in_specs=[
            pl.BlockSpec((1, 1, dh, cols), lambda h, i: (h, i, 0, 0)),
            pl.BlockSpec((t, dh), lambda h, i: (0, h)),
            pl.BlockSpec((1, nblk, dh, tq), lambda h, i: (h, 0, 0, 0)),
        ],
        out_specs=pl.BlockSpec((tq, GQA_GROUP * dh), lambda h, i: (i, h)),
        out_shape=jax.ShapeDtypeStruct((t, n_kv * GQA_GROUP * dh), BF16),
        scratch_shapes=[pltpu.VMEM((1, cols), F32), pltpu.VMEM((1, cols), F32), pltpu.VMEM((dh, cols), F32),
                        pltpu.VMEM((kc, cols), F32), pltpu.VMEM((kc, cols), F32)],
        compiler_params=_cparams(("arbitrary", "arbitrary"), 4 * t * dh * 2 + 8 * kc * cols * 4 + 4 * dh * cols * 4),
        name="attention",
    )(qt, k, vt)


def _mlstm_kernel(q_ref, k_ref, v_ref, g_ref, b_ref, h_ref, ct_sc, m_sc, *, n_heads, dk, dv):
    ln = q_ref.shape[0]
    dh_id = pl.program_id(0)
    direction = dh_id // n_heads
    head = dh_id % n_heads
    col_i = direction * 2 * n_heads + head
    col_f = col_i + n_heads

    @pl.when(pl.program_id(1) == 0)
    def _():
        ct_sc[...] = jnp.zeros_like(ct_sc)
        m_sc[...] = jnp.zeros_like(m_sc)

    pre = g_ref[...] + b_ref[...]
    pre = GATE_SOFTCAP * jnp.tanh(pre / GATE_SOFTCAP)
    lane = lax.broadcasted_iota(jnp.int32, pre.shape, 1)
    is_forget = ((lane // n_heads) % 2) == 1
    gates = jnp.where(is_forget, jax.nn.log_sigmoid(pre), pre)

    r = lax.broadcasted_iota(jnp.int32, (ln, ln), 0)
    c = lax.broadcasted_iota(jnp.int32, (ln, ln), 1)
    allowed = jnp.where(direction == 0, r - c, c - r) >= 0
    cum = jnp.dot(allowed.astype(F32), gates, preferred_element_type=F32, precision=lax.Precision.HIGHEST)

    def pick_col(a, idx):
        return jnp.sum(jnp.where(lane == idx, a, 0.0), axis=1, keepdims=True)

    sub = lax.broadcasted_iota(jnp.int32, (LANES, ln), 0)

    def pick_row(a, idx):
        return jnp.sum(jnp.where(sub == idx, a.T, 0.0), axis=0, keepdims=True)

    b_col = pick_col(cum, col_f)
    i_col = pick_col(gates, col_i)
    b_row = pick_row(cum, col_f)
    i_row = pick_row(gates, col_i)
    m_prev = m_sc[...]

    log_d = jnp.where(allowed, b_col - b_row + i_row, -jnp.inf)
    log_inter = b_col + m_prev
    m_t = jnp.maximum(log_inter, jnp.max(log_d, axis=1, keepdims=True))
    scale = dk ** -0.5
    d_mat = jnp.exp(log_d - m_t) * scale
    w_inter = jnp.exp(log_inter - m_t) * scale

    q = q_ref[...]
    k = k_ref[...]
    v_aug = jnp.concatenate([v_ref[...], jnp.ones((ln, LANES), BF16)], axis=1)
    s = lax.dot_general(q, k, (((1,), (1,)), ((), ())), preferred_element_type=F32) * d_mat
    intra = jnp.dot(s.astype(BF16), v_aug, preferred_element_type=F32)
    ct = ct_sc[...]
    inter = jnp.dot(q, ct.astype(BF16), preferred_element_type=F32)
    nd = intra + w_inter * inter
    den = nd[:, dv:dv + 1]
    h_ref[0] = nd[:, :dv] / jnp.maximum(jnp.abs(den), jnp.exp(-m_t))

    g_tot = jnp.sum(pick_col(gates, col_f), axis=0, keepdims=True)
    log_w = g_tot - b_col + i_col
    m_new = jnp.maximum(g_tot + m_prev, jnp.max(log_w, axis=0, keepdims=True))
    w = jnp.exp(log_w - m_new)
    decay = jnp.exp(g_tot + m_prev - m_new)
    wv = (w * v_aug.astype(F32)).astype(BF16)
    upd = lax.dot_general(k, wv, (((0,), (0,)), ((), ())), preferred_element_type=F32)
    ct_sc[...] = decay * ct + upd
    m_sc[...] = m_new


def mlstm(y, gates_raw, gate_bias, q_col0, k_col0, v_col0, dk, dv):
    t = y.shape[0]
    nh = MLSTM_HEADS
    ln = MLSTM_CHUNK
    nblk = t // ln
    bias = jnp.zeros((1, LANES), F32).at[0, :4 * nh].set(gate_bias)

    def blk(dhid, j):
        return jnp.where(dhid // nh == 0, j, jnp.where(j == 0, 0, nblk - j))

    return pl.pallas_call(
        functools.partial(_mlstm_kernel, n_heads=nh, dk=dk, dv=dv),
        grid=(2 * nh, nblk),
        in_specs=[
            pl.BlockSpec((ln, dk), lambda d, j: (blk(d, j), q_col0 // dk + d % nh)),
            pl.BlockSpec((ln, dk), lambda d, j: (blk(d, j), k_col0 // dk + d % nh)),
            pl.BlockSpec((ln, dv), lambda d, j: (blk(d, j), v_col0 // dv + d % nh)),
            pl.BlockSpec((ln, LANES), lambda d, j: (blk(d, j), 0)),
            pl.BlockSpec((1, LANES), lambda d, j: (0, 0)),
        ],
        out_specs=pl.BlockSpec((1, ln, dv), lambda d, j: (d // nh, blk(d, j), d % nh)),
        out_shape=jax.ShapeDtypeStruct((2, t, nh * dv), F32),
        scratch_shapes=[pltpu.VMEM((dk, dv + LANES), F32), pltpu.VMEM((1, 1), F32)],
        compiler_params=_cparams(("arbitrary", "arbitrary"), 32 * 2**20),
        name="mlstm_scan",
    )(y, y, y, gates_raw, bias)


def _mlstm_out_kernel(h_ref, mo_ref, g_ref, o_ref):
    h = h_ref[0] + h_ref[1]
    hn = h * lax.rsqrt(jnp.mean(h * h, axis=-1, keepdims=True) + EPS) * g_ref[...]
    o_ref[...] = (hn * jax.nn.sigmoid(mo_ref[...].astype(F32))).astype(BF16)


def mlstm_output(h2, y, mlstm_norm, mo_col0, dv):
    t = y.shape[0]
    nh = MLSTM_HEADS
    tm = ROW_TILE
    return pl.pallas_call(
        _mlstm_out_kernel,
        grid=(t // tm, nh),
        in_specs=[
            pl.BlockSpec((2, tm, dv), lambda i, h: (0, i, h)),
            pl.BlockSpec((tm, dv), lambda i, h: (i, mo_col0 // dv + h)),
            pl.BlockSpec((1, dv), lambda i, h: (0, h)),
        ],
        out_specs=pl.BlockSpec((tm, dv), lambda i, h: (i, h)),
        out_shape=jax.ShapeDtypeStruct((t, nh * dv), BF16),
        compiler_params=_cparams(("arbitrary", "arbitrary"), 16 * tm * dv * 4),
        name="mlstm_out",
    )(h2, y, mlstm_norm.reshape(1, nh * dv))


def _out_proj_kernel(a_ref, m_ref, w1_ref, w2_ref, h_ref, mod_ref, o_ref, w1_sc, w2_sc, *, n_ctx, tm, gate_idx):
    @pl.when(pl.program_id(1) == 0)
    def _():
        w1_sc[...] = w1_ref[...].astype(BF16)
        w2_sc[...] = w2_ref[...].astype(BF16)

    acc = jnp.dot(a_ref[...], w1_sc[...], preferred_element_type=F32)
    acc = acc + jnp.dot(m_ref[...], w2_sc[...], preferred_element_type=F32)
    rows = pl.program_id(1) * tm + lax.broadcasted_iota(jnp.int32, (tm, 1), 0)
    o_ref[...] = h_ref[...] + jnp.tanh(_stream_vec(mod_ref, gate_idx, rows, n_ctx)) * acc


def out_projection(a, m, w_out, h, mod, n_ctx, gate_idx):
    t, half = a.shape
    d = w_out.shape[1]
    tm = _pick_tile(t, (768, 512, 256))
    tn = _pick_tile(d, (512, 256, 128))
    return pl.pallas_call(
        functools.partial(_out_proj_kernel, n_ctx=n_ctx, tm=tm, gate_idx=gate_idx),
        grid=(d // tn, t // tm),
        in_specs=[
            pl.BlockSpec((tm, half), lambda j, i: (i, 0)),
            pl.BlockSpec((tm, half), lambda j, i: (i, 0)),
            pl.BlockSpec((half, tn), lambda j, i: (0, j)),
            pl.BlockSpec((half, tn), lambda j, i: (1, j)),
            pl.BlockSpec((tm, tn), lambda j, i: (i, j)),
            pl.BlockSpec((2, 6, tn), lambda j, i: (0, 0, j)),
        ],
        out_specs=pl.BlockSpec((tm, tn), lambda j, i: (i, j)),
        out_shape=jax.ShapeDtypeStruct((t, d), F32),
        scratch_shapes=[pltpu.VMEM((half, tn), BF16), pltpu.VMEM((half, tn), BF16)],
        compiler_params=_cparams(("arbitrary", "arbitrary"),
                                 2 * (2 * tm * half * 2 + 2 * half * tn * 4 + 2 * tm * tn * 4) + 2 * half * tn * 2
                                 + tm * tn * 4),
        name="out_proj",
    )(a, m, w_out, w_out, h, mod)


def _pool_kernel(h_ref, hp_ref, hn_ref, g_ref, mod_ref, o_ref, z_sc, *, seq, tm):
    i = pl.program_id(0)
    nblk = pl.num_programs(0)
    halo = POOL_HALO
    d = h_ref.shape[1]
    gd = d // len(POOL_WINDOWS)

    def normed(x):
        y = x * lax.rsqrt(jnp.mean(x * x, axis=-1, keepdims=True) + EPS) * g_ref[...]
        return y * (1.0 + mod_ref[0, 1:2, :]) + mod_ref[0, 0:1, :]

    z_sc[0:halo, :] = jnp.where(i > 0, normed(hp_ref[...]), 0.0)
    z_sc[halo:halo + tm, :] = normed(h_ref[...])
    z_sc[halo + tm:2 * halo + tm, :] = jnp.where(i < nblk - 1, normed(hn_ref[...]), 0.0)

    t = i * tm + lax.broadcasted_iota(jnp.int32, (tm, 1), 0)
    for gi, w in enumerate(POOL_WINDOWS):
        left = w // 2
        right = w - 1 - left
        cols = slice(gi * gd, (gi + 1) * gd)
        acc = z_sc[halo - left:halo - left + tm, cols]
        for off in range(-left + 1, right + 1):
            acc = acc + z_sc[halo + off:halo + off + tm, cols]
        cnt = (jnp.minimum(t + right + 1, seq) - jnp.maximum(t - left, 0)).astype(F32)
        o_ref[:, cols] = (acc / cnt - z_sc[halo:halo + tm, cols]).astype(BF16)


def pool_features(h, g, mod):
    seq, d = h.shape
    tm = ROW_TILE
    halo = POOL_HALO
    r = tm // halo
    nhb = seq // halo
    return pl.pallas_call(
        functools.partial(_pool_kernel, seq=seq, tm=tm),
        grid=(seq // tm,),
        in_specs=[
            pl.BlockSpec((tm, d), lambda i: (i, 0)),
            pl.BlockSpec((halo, d), lambda i: (jnp.maximum(i * r - 1, 0), 0)),
            pl.BlockSpec((halo, d), lambda i: (jnp.minimum((i + 1) * r, nhb - 1), 0)),
            pl.BlockSpec((1, d), lambda i: (0, 0)),
            pl.BlockSpec((2, 6, d), lambda i: (0, 0, 0)),
        ],
        out_specs=pl.BlockSpec((tm, d), lambda i: (i, 0)),
        out_shape=jax.ShapeDtypeStruct((seq, d), BF16),
        scratch_shapes=[pltpu.VMEM((tm + 2 * halo, d), F32)],
        compiler_params=_cparams(("arbitrary",), 10 * tm * d * 4),
        name="pool_features",
    )(h, h, h, g.reshape(1, d), mod)


def _pool_proj_kernel(x_ref, w_ref, ps_ref, h_ref, mod_ref, o_ref):
    acc = jnp.dot(x_ref[...], w_ref[0], preferred_element_type=F32)
    o_ref[...] = h_ref[...] + jnp.tanh(mod_ref[0, 2:3, :]) * (acc * ps_ref[...])


def pool_projection(dp, pool_w, pool_scale, h, mod):
    seq, d = h.shape
    ng, gd, _ = pool_w.shape
    tm = _pick_tile(seq, (1024, 512, 256))
    return pl.pallas_call(
        _pool_proj_kernel,
        grid=(ng, seq // tm),
        in_specs=[
            pl.BlockSpec((tm, gd), lambda g, i: (i, g)),
            pl.BlockSpec((1, gd, gd), lambda g, i: (g, 0, 0)),
            pl.BlockSpec((1, gd), lambda g, i: (0, g)),
            pl.BlockSpec((tm, gd), lambda g, i: (i, g)),
            pl.BlockSpec((2, 6, gd), lambda g, i: (0, 0, g)),
        ],
        out_specs=pl.BlockSpec((tm, gd), lambda g, i: (i, g)),
        out_shape=jax.ShapeDtypeStruct((seq, d), F32),
        compiler_params=_cparams(("arbitrary", "arbitrary"), 2 * (tm * gd * 10 + gd * gd * 2) + tm * gd * 4),
        name="pool_proj",
    )(dp, pool_w, pool_scale.reshape(1, d), h, mod)


def _route_kernel(lg_ref, bias_ref, gate_ref, idx_ref, rank_ref, cnt_ref, run_sc, *, n_exp):
    ng = N_GROUPS
    ge = n_exp // ng
    tt = lg_ref.shape[1]
    neg = -jnp.inf

    @pl.when(pl.program_id(0) == 0)
    def _():
        run_sc[...] = jnp.zeros_like(run_sc)

    scores = jax.nn.sigmoid(lg_ref[...])
    sel = (scores + bias_ref[...]).reshape(ng, ge, tt)
    e_in_g = lax.broadcasted_iota(jnp.int32, (ng, ge, tt), 1)
    m1 = jnp.max(sel, axis=1, keepdims=True)
    first = jnp.min(jnp.where(sel == m1, e_in_g, ge), axis=1, keepdims=True)
    m2 = jnp.max(jnp.where(e_in_g == first, neg, sel), axis=1, keepdims=True)
    gscore = m1 + m2

    gid = lax.broadcasted_iota(jnp.int32, (ng, 1, tt), 0)
    gmask = jnp.zeros((ng, 1, tt), jnp.bool_)
    for _ in range(TOPK_GROUPS):
        mx = jnp.max(gscore, axis=0, keepdims=True)
        pick = gid == jnp.min(jnp.where(gscore == mx, gid, ng), axis=0, keepdims=True)
        gmask = jnp.logical_or(gmask, pick)
        gscore = jnp.where(pick, neg, gscore)

    eid = lax.broadcasted_iota(jnp.int32, (ng, ge, tt), 0) * ge + e_in_g
    cand = jnp.where(gmask, sel, neg)
    chosen = jnp.zeros((ng, ge, tt), jnp.bool_)
    for kk in range(TOP_K):
        mx = jnp.max(jnp.max(cand, axis=1, keepdims=True), axis=0, keepdims=True)
        hit = jnp.where(cand == mx, eid, n_exp)
        pick_id = jnp.min(jnp.min(hit, axis=1, keepdims=True), axis=0, keepdims=True)
        pick = eid == pick_id
        chosen = jnp.logical_or(chosen, pick)
        cand = jnp.where(pick, neg, cand)
        idx_ref[kk:kk + 1, :] = pick_id.reshape(1, tt)

    s3 = scores.reshape(ng, ge, tt)
    w = jnp.where(chosen, s3, 0.0)
    denom = jnp.sum(jnp.sum(w, axis=1, keepdims=True), axis=0, keepdims=True)
    gate_ref[...] = (w / denom * ROUTED_SCALE).reshape(n_exp, tt)

    chosen_f = chosen.reshape(n_exp, tt).astype(F32)
    earlier = (lax.broadcasted_iota(jnp.int32, (tt, tt), 0) < lax.broadcasted_iota(jnp.int32, (tt, tt), 1))
    local = jnp.dot(chosen_f.astype(BF16), earlier.astype(BF16), preferred_element_type=F32)
    run = run_sc[...]
    rank_ref[...] = jnp.where(chosen.reshape(n_exp, tt), local + run, -1.0).astype(jnp.int32)
    run = run + jnp.sum(chosen_f, axis=1, keepdims=True)
    run_sc[...] = run
    cnt_ref[...] = jnp.broadcast_to(run, cnt_ref.shape).astype(jnp.int32)


def route(logits_t, router_bias):
    n_exp, t = logits_t.shape
    tt = _pick_tile(t, (512, 256, 128))
    gates, idx, rank, cnt = pl.pallas_call(
        functools.partial(_route_kernel, n_exp=n_exp),
        grid=(t // tt,),
        in_specs=[pl.BlockSpec((n_exp, tt), lambda i: (0, i)), pl.BlockSpec((n_exp, 1), lambda i: (0, 0))],
        out_specs=[
            pl.BlockSpec((n_exp, tt), lambda i: (0, i)),
            pl.BlockSpec((TOP_K, tt), lambda i: (0, i)),
            pl.BlockSpec((n_exp, tt), lambda i: (0, i)),
            pl.BlockSpec((n_exp, LANES), lambda i: (0, 0)),
        ],
        out_shape=[
            jax.ShapeDtypeStruct((n_exp, t), F32),
            jax.ShapeDtypeStruct((TOP_K, t), jnp.int32),
            jax.ShapeDtypeStruct((n_exp, t), jnp.int32),
            jax.ShapeDtypeStruct((n_exp, LANES), jnp.int32),
        ],
        scratch_shapes=[pltpu.VMEM((n_exp, 1), F32)],
        compiler_params=_cparams(("arbitrary",), 32 * 2**20),
        name="moe_route",
    )(logits_t, router_bias.reshape(n_exp, 1))
    return gates, idx, rank, cnt[:, 0]


def _slab_copy(src, src_row, dst, dst_row, sem):
    return pltpu.make_async_copy(src.at[src_row], dst.at[dst_row], sem)


def _dispatch_kernel(pos_ref, pad_ref, zp_ref, xs_ref, zero_sc, sem, pad_sem, *, tt, n_exp, tile, n_tiles):
    i = pl.program_id(0)

    def for_each_zero_copy(fn):
        def per_expert(e, carry):
            first = pad_ref[0, e]

            def one(r, c):
                fn(pltpu.make_async_copy(zero_sc.at[0], xs_ref.at[first + r], pad_sem))
                return c

            lax.fori_loop(0, pad_ref[1, e], one, 0)
            return carry

        lax.fori_loop(0, n_exp, per_expert, 0)

        def tail(tile_idx, c):
            fn(pltpu.make_async_copy(zero_sc, xs_ref.at[pl.ds(tile_idx * tile, tile)], pad_sem))
            return c

        lax.fori_loop(pad_ref[2, 0], n_tiles, tail, 0)

    @pl.when(i == 0)
    def _():
        zero_sc[...] = jnp.zeros_like(zero_sc)
        for_each_zero_copy(lambda cp: cp.start())

    def issue(j, carry):
        for k in range(TOP_K):
            _slab_copy(zp_ref, j, xs_ref, pos_ref[0, 0, j * TOP_K + k], sem).start()
        return carry

    lax.fori_loop(0, tt, issue, 0)
    for _ in range(tt * TOP_K):
        _slab_copy(zp_ref, 0, xs_ref, 0, sem).wait()

    @pl.when(i == pl.num_programs(0) - 1)
    def _():
        for_each_zero_copy(lambda cp: cp.wait())


def dispatch(zp, pos_tk, pad_info, n_tiles):
    t, ns, _ = zp.shape
    tt = COMBINE_TILE
    tile = EXPERT_TILE
    n_exp = pad_info.shape[1]
    return pl.pallas_call(
        functools.partial(_dispatch_kernel, tt=tt, n_exp=n_exp, tile=tile, n_tiles=n_tiles),
        grid=(t // tt,),
        in_specs=[
            pl.BlockSpec((1, 1, tt * TOP_K), lambda i: (i, 0, 0), memory_space=pltpu.SMEM),
            pl.BlockSpec(memory_space=pltpu.SMEM),
            pl.BlockSpec((tt, ns, LANES), lambda i: (i, 0, 0)),
        ],
        out_specs=pl.BlockSpec(memory_space=pl.ANY),
        out_shape=jax.ShapeDtypeStruct((n_tiles * tile, ns, LANES), jnp.uint32),
        scratch_shapes=[pltpu.VMEM((tile, ns, LANES), jnp.uint32), pltpu.SemaphoreType.DMA(()),
                        pltpu.SemaphoreType.DMA(())],
        compiler_params=_cparams(("arbitrary",), (2 * tt + tile) * ns * LANES * 4),
        name="moe_dispatch",
    )(pos_tk.reshape(t // tt, 1, tt * TOP_K), pad_info, zp)


def _expert_kernel(te_ref, first_ref, nused_ref, x_ref, wg_ref, wu_ref, wd_ref, y_ref, wg_sc, wu_sc, wd_sc):
    i = pl.program_id(0)

    @pl.when(i < nused_ref[0])
    def _():
        @pl.when(first_ref[i] == 1)
        def _():
            wg_sc[...] = wg_ref[0, 0].astype(BF16)
            wu_sc[...] = wu_ref[0, 0].astype(BF16)
            wd_sc[...] = wd_ref[0, 0].astype(BF16)

        x = _unpack_rows(_load_slabs(x_ref)).astype(BF16)
        g = jnp.dot(x, wg_sc[...], preferred_element_type=F32)
        u = jnp.dot(x, wu_sc[...], preferred_element_type=F32)
        hid = (g * jax.nn.sigmoid(g) * u).astype(BF16)
        y = jnp.dot(hid, wd_sc[...], preferred_element_type=F32)
        _store_slabs(y_ref, _pack_rows(y))

    @pl.when(i >= nused_ref[0])
    def _():
        y_ref[...] = jnp.zeros_like(y_ref)


def expert_ffn(xs, tile_expert, tile_first, n_used, layer, exp_gate, exp_up, exp_down):
    n_rows, ns, _ = xs.shape
    _, n_exp, d, f = exp_gate.shape
    tm = EXPERT_TILE
    n_tiles = n_rows // tm

    def row_blk(i, te, first, nused):
        return (jnp.minimum(i, nused[0] - 1), 0, 0)

    def w_blk(i, te, first, nused):
        return (layer, te[i], 0, 0)

    grid_spec = pltpu.PrefetchScalarGridSpec(
        num_scalar_prefetch=3,
        grid=(n_tiles,),
        in_specs=[
            pl.BlockSpec((tm, ns, LANES), row_blk),
            pl.BlockSpec((1, 1, d, f), w_blk),
            pl.BlockSpec((1, 1, d, f), w_blk),
            pl.BlockSpec((1, 1, f, d), w_blk),
        ],
        out_specs=pl.BlockSpec((tm, ns, LANES), lambda i, te, first, nused: (i, 0, 0)),
        scratch_shapes=[pltpu.VMEM((d, f), BF16), pltpu.VMEM((d, f), BF16), pltpu.VMEM((f, d), BF16)],
    )
    return pl.pallas_call(
        _expert_kernel,
        grid_spec=grid_spec,
        out_shape=jax.ShapeDtypeStruct((n_rows, ns, LANES), jnp.uint32),
        compiler_params=_cparams(("arbitrary",), 2 * 3 * d * f * 4 + 3 * d * f * 2 + 4 * tm * ns * LANES * 4
                                 + 6 * tm * d * 4),
        name="moe_experts",
    )(tile_expert, tile_first, n_used, xs, exp_gate, exp_up, exp_down)


def _combine_kernel(pos_ref, pos_next_ref, ys_ref, wt_ref, z_ref, sg_ref, su_ref, sd_ref, h_ref, mod_ref, o_ref,
                    buf, sem, *, tt, n_ctx, row0):
    i = pl.program_id(0)
    slot = i % 2

    def issue(p_ref, s):
        def body(j, carry):
            for k in range(TOP_K):
                _slab_copy(ys_ref, p_ref[0, 0, j * TOP_K + k], buf.at[s, k], j, sem.at[s]).start()
            return carry

        lax.fori_loop(0, tt, body, 0)

    @pl.when(i == 0)
    def _():
        issue(pos_ref, 0)

    @pl.when(i + 1 < pl.num_programs(0))
    def _():
        issue(pos_next_ref, 1 - slot)

    z = z_ref[...]
    g = jnp.dot(z, sg_ref[...], preferred_element_type=F32)
    u = jnp.dot(z, su_ref[...], preferred_element_type=F32)
    acc = jnp.dot((g * jax.nn.sigmoid(g) * u).astype(BF16), sd_ref[...], preferred_element_type=F32)

    for _ in range(tt * TOP_K):
        _slab_copy(ys_ref, 0, buf.at[slot, 0], 0, sem.at[slot]).wait()

    wt = wt_ref[...]
    for k in range(TOP_K):
        acc = acc + wt[:, k:k + 1] * _unpack_rows(_load_slabs(buf.at[slot, k]))
    rows = row0 + pl.program_id(0) * tt + lax.broadcasted_iota(jnp.int32, (tt, 1), 0)
    o_ref[...] = h_ref[...] + jnp.tanh(_stream_vec(mod_ref, 5, rows, n_ctx)) * acc


def combine(ys, pos_tk, w_tk, z, sh_gate, sh_up, sh_down, h, mod, n_ctx, row0):
    t, d = h.shape
    tt = COMBINE_TILE
    f = sh_gate.shape[1]
    ns = ys.shape[1]
    b0 = row0 // tt
    n_out = t - row0
    last = t // tt - 1
    pos_blocks = pos_tk.reshape(t // tt, 1, tt * TOP_K)
    return pl.pallas_call(
        functools.partial(_combine_kernel, tt=tt, n_ctx=n_ctx, row0=row0),
        grid=(n_out // tt,),
        in_specs=[
            pl.BlockSpec((1, 1, tt * TOP_K), lambda i: (b0 + i, 0, 0), memory_space=pltpu.SMEM),
            pl.BlockSpec((1, 1, tt * TOP_K), lambda i: (jnp.minimum(b0 + i + 1, last), 0, 0), memory_space=pltpu.SMEM),
            pl.BlockSpec(memory_space=pl.ANY),
            pl.BlockSpec((tt, TOP_K), lambda i: (b0 + i, 0)),
            pl.BlockSpec((tt, d), lambda i: (b0 + i, 0)),
            pl.BlockSpec((d, f), lambda i: (0, 0)),
            pl.BlockSpec((d, f), lambda i: (0, 0)),
            pl.BlockSpec((f, d), lambda i: (0, 0)),
            pl.BlockSpec((tt, d), lambda i: (b0 + i, 0)),
            pl.BlockSpec((2, 6, d), lambda i: (0, 0, 0)),
        ],
        out_specs=pl.BlockSpec((tt, d), lambda i: (i, 0)),
        out_shape=jax.ShapeDtypeStruct((n_out, d), F32),
        scratch_shapes=[pltpu.VMEM((2, TOP_K, tt, ns, LANES), jnp.uint32), pltpu.SemaphoreType.DMA((2,))],
        compiler_params=_cparams(("arbitrary",),
                                 2 * TOP_K * tt * ns * LANES * 4 + 12 * tt * d * 4 + 6 * d * f * 2 * 2),
        name="moe_combine",
    )(pos_blocks, pos_blocks, ys, w_tk, z, sh_gate, sh_up, sh_down, h, mod)


def moe_block(h, norm_g, mod, n_ctx, row0, layer, router_w, router_bias, exp_gate, exp_up, exp_down,
              sh_gate, sh_up, sh_down):
    t, d = h.shape
    n_exp = router_w.shape[1]
    z, zp, logits = norm_modulate(h, norm_g, mod, n_ctx, 3, router_w=router_w)
    gates, idx_t, rank, counts = route(logits[:, :n_exp].T, router_bias)

    tm = EXPERT_TILE
    padded = (counts + tm - 1) // tm * tm
    ends = jnp.cumsum(padded)
    starts = ends - padded
    n_tiles = (t * TOP_K) // tm + n_exp
    tile_start = jnp.arange(n_tiles, dtype=jnp.int32) * tm
    n_used = (ends[-1] // tm).astype(jnp.int32)
    tile_expert = jnp.sum((ends[None, :] <= tile_start[:, None]).astype(jnp.int32), axis=1)
    tile_expert = jnp.minimum(tile_expert, n_exp - 1)
    tile_expert = jnp.where(tile_start < ends[-1], tile_expert, tile_expert[jnp.maximum(n_used - 1, 0)])
    tile_first = jnp.concatenate([jnp.ones((1,), jnp.int32), (tile_expert[1:] != tile_expert[:-1]).astype(jnp.int32)])

    pos_dense = starts[:, None] + rank
    pos_tk = jnp.take_along_axis(pos_dense, idx_t, axis=0).T.astype(jnp.int32)
    w_tk = jnp.take_along_axis(gates, idx_t, axis=0).T

    pad_info = jnp.stack([starts + counts, padded - counts, jnp.full_like(counts, n_used)]).astype(jnp.int32)
    xs = dispatch(zp, pos_tk, pad_info, n_tiles)
    ys = expert_ffn(xs, tile_expert, tile_first, n_used.reshape(1), layer, exp_gate, exp_up, exp_down)
    return combine(ys, pos_tk, w_tk, z, sh_gate.astype(BF16), sh_up.astype(BF16), sh_down.astype(BF16),
                   h, mod, n_ctx, row0)


def mixer_layer(h, n_ctx, mod, norm_g, w_in, gate_bias, q_norm, k_norm, mlstm_norm, w_out, cos_t, sin_t):
    t, d = h.shape
    dh = ATTN_HEAD_DIM
    n_q = (d // 2) // dh
    n_kv = n_q // GQA_GROUP
    nh = MLSTM_HEADS
    dv = (d // 2) // nh
    dk = dv // 2
    sizes = (n_q * dh, n_kv * dh, n_kv * dh, nh * dk, nh * dk, nh * dv, nh * dv, 4 * nh)
    off = [0]
    for s in sizes:
        off.append(off[-1] + s)
    n_main = off[7]

    z = norm_modulate(h, norm_g, mod, n_ctx, 0)
    y = matmul(z, w_in, n_main, BF16)
    wg = jnp.zeros((d, LANES), F32).at[:, :4 * nh].set(w_in[:, n_main:])
    gates_raw = matmul(z, wg, LANES, F32)

    qt, k, vt = qk_prepare(y, cos_t, sin_t, q_norm, k_norm, n_q, n_kv)
    a = attention(qt, k, vt, n_ctx)
    h2 = mlstm(y, gates_raw, gate_bias, off[3], off[4], off[5], dk, dv)
    m = mlstm_output(h2, y, mlstm_norm, off[6], dv)
    return out_projection(a, m, w_out, h, mod, n_ctx, 2)


def kernel(x, c, ctx, c_ctx, ada_w, ada_b, norm_mix, norm_ffn, w_in, gate_bias, q_norm, k_norm, mlstm_norm, w_out,
           pool_w, pool_scale, router_w, router_bias, exp_gate, exp_up, exp_down, sh_gate, sh_up, sh_down):
    depth = ada_w.shape[0]
    seq = x.shape[1]
    n_ctx = ctx.shape[1]
    assert x.shape[0] == 1 and n_ctx == ROW_TILE and seq % ROW_TILE == 0
    mods = ada_modulation(c, c_ctx, ada_w, ada_b)
    cos_t, sin_t = rope_tables(n_ctx, seq)

    h = jnp.concatenate([ctx[0], x[0]], axis=0)
    nc = n_ctx
    for layer in range(depth):
        last = layer == depth - 1
        j = layer // 2
        mod = mods[layer]
        if layer % 2 == 0:
            h = mixer_layer(h, nc, mod, norm_mix[layer], w_in[j], gate_bias[j], q_norm[j], k_norm[j],
                            mlstm_norm[j], w_out[j], cos_t, sin_t)
        else:
            assert nc == 0 or not last
            dp = pool_features(h, norm_mix[layer], mod)
            h = pool_projection(dp, pool_w[j].astype(BF16), pool_scale[j], h, mod)
        drop_ctx = nc > 0 and (last or (layer + 1 == depth - 1 and (depth - 1) % 2 == 1))
        row0 = nc if drop_ctx else 0
        h = moe_block(h, norm_ffn[layer], mod, nc, row0, layer, router_w[layer], router_bias[layer], exp_gate,
                      exp_up, exp_down, sh_gate[layer], sh_up[layer], sh_down[layer])
        if drop_ctx:
            nc = 0
    return h[nc:][None]
```

```python
import functools

import jax
import jax.numpy as jnp
from jax import lax
from jax.experimental import pallas as pl
from jax.experimental.pallas import tpu as pltpu

F32 = jnp.float32
BF16 = jnp.bfloat16

EPS = 1e-6
ATTN_HEAD_DIM = 128
GQA_GROUP = 4
ROPE_THETA = 10000.0
GRID_W = 64
MLSTM_HEADS = 4
GATE_SOFTCAP = 15.0
POOL_WINDOWS = (2, 4, 8, 16)
POOL_HALO = 16
TOP_K = 8
N_GROUPS = 8
TOPK_GROUPS = 4
ROUTED_SCALE = 2.5

LANES = 128
V7X_VMEM_BUDGET = 56 * 2**20
COMPILER_TEMP_BYTES = 8 * 2**20
ROW_TILE = 256
MLSTM_CHUNK = 256
ATTN_KV_CHUNK = 512
EXPERT_TILE = 256
COMBINE_TILE = 128


def _cparams(sem, vmem_bytes):
    limit = min(vmem_bytes + COMPILER_TEMP_BYTES, V7X_VMEM_BUDGET)
    return pltpu.CompilerParams(dimension_semantics=sem, vmem_limit_bytes=int(limit))


def _pick_tile(n, candidates):
    for c in candidates:
        if n % c == 0:
            return c
    raise ValueError(f"no tile for {n}")


ADA_K_CHUNK = 512


def _ada_kernel(s_ref, w_ref, b_ref, o_ref):
    d = w_ref.shape[1]
    tn = w_ref.shape[2]
    sub = 8

    def body(kc, acc):
        k0 = pl.multiple_of(kc * ADA_K_CHUNK, ADA_K_CHUNK)
        s = s_ref[pl.ds(k0, ADA_K_CHUNK), :]
        s = (s * jax.nn.sigmoid(s)).reshape(ADA_K_CHUNK // sub, sub, LANES)
        w = w_ref[0, pl.ds(k0, ADA_K_CHUNK), :].reshape(ADA_K_CHUNK // sub, sub, tn)
        return tuple(a + jnp.sum(w * s[:, :, j:j + 1], axis=0) for j, a in enumerate(acc))

    acc = lax.fori_loop(0, d // ADA_K_CHUNK, body, (jnp.zeros((sub, tn), F32), jnp.zeros((sub, tn), F32)))
    for j, a in enumerate(acc):
        o_ref[0, j:j + 1, :] = jnp.sum(a, axis=0, keepdims=True) + b_ref[0]


def ada_modulation(c, c_ctx, ada_w, ada_b):
    depth, d, n = ada_w.shape
    s = jnp.zeros((d, LANES), F32).at[:, 0].set(c[0]).at[:, 1].set(c_ctx)
    tn = _pick_tile(n, (1024, 512, 256, 128))
    out = pl.pallas_call(
        _ada_kernel,
        grid=(depth, n // tn),
        in_specs=[
            pl.BlockSpec((d, LANES), lambda l, j: (0, 0)),
            pl.BlockSpec((1, d, tn), lambda l, j: (l, 0, j)),
            pl.BlockSpec((1, 1, tn), lambda l, j: (l, 0, j)),
        ],
        out_specs=pl.BlockSpec((1, 2, tn), lambda l, j: (l, 0, j)),
        out_shape=jax.ShapeDtypeStruct((depth, 2, n), F32),
        compiler_params=_cparams(("arbitrary", "arbitrary"), 2 * d * tn * 4 + 8 * ADA_K_CHUNK * tn * 4),
        name="ada_mod",
    )(s, ada_w, ada_b.reshape(depth, 1, n))
    return out.reshape(depth, 2, 6, d)


def _stream_vec(mod_ref, idx, rows, n_ctx):
    lat = mod_ref[0, idx:idx + 1, :]
    if n_ctx == 0:
        return lat
    return jnp.where(rows < n_ctx, mod_ref[1, idx:idx + 1, :], lat)


def _pack_rows(x):
    half = x.shape[1] // 2
    return pltpu.pack_elementwise([x[:, :half], x[:, half:]], packed_dtype=BF16)


def _unpack_rows(xp):
    lo = pltpu.unpack_elementwise(xp, index=0, packed_dtype=BF16, unpacked_dtype=F32)
    hi = pltpu.unpack_elementwise(xp, index=1, packed_dtype=BF16, unpacked_dtype=F32)
    return jnp.concatenate([lo, hi], axis=1)


SUBLANES = 8


def _sublane_transpose(v):
    ax = v[0].ndim - 2
    sub = lax.broadcasted_iota(jnp.int32, v[0].shape, ax)
    for b in (4, 2, 1):
        upper = (sub & b) != 0
        nxt = []
        for a in range(SUBLANES):
            other = v[a ^ b]
            if a & b == 0:
                nxt.append(jnp.where(upper, pltpu.roll(other, b, axis=ax), v[a]))
            else:
                nxt.append(jnp.where(upper, v[a], pltpu.roll(other, SUBLANES - b, axis=ax)))
        v = nxt
    return v


def _store_slabs(ref, xp):
    rows, ns, _ = ref.shape
    x = pltpu.bitcast(xp, jnp.int32)
    u = [x[:, c * LANES:(c + 1) * LANES].reshape(rows // SUBLANES, SUBLANES, LANES) for c in range(ns)]
    v = [jnp.stack([u[SUBLANES * h + s] for h in range(ns // SUBLANES)], axis=1) for s in range(SUBLANES)]
    w = _sublane_transpose(v)
    ref[...] = pltpu.bitcast(jnp.stack(w, axis=1).reshape(rows, ns, LANES), jnp.uint32)


def _load_slabs(ref):
    rows, ns, _ = ref.shape
    x = pltpu.bitcast(ref[...], jnp.int32).reshape(rows // SUBLANES, SUBLANES, ns // SUBLANES, SUBLANES, LANES)
    v = _sublane_transpose([x[:, j] for j in range(SUBLANES)])
    cols = [v[s][:, h].reshape(rows, LANES) for h in range(ns // SUBLANES) for s in range(SUBLANES)]
    return pltpu.bitcast(jnp.concatenate(cols, axis=1), jnp.uint32)


def _norm_mod_kernel(h_ref, g_ref, mod_ref, *rest, n_ctx, tm, shift_idx, with_router):
    if with_router:
        rw_ref, z_ref, zp_ref, lg_ref = rest
    else:
        (z_ref,) = rest
    x = h_ref[...]
    rows = pl.program_id(0) * tm + lax.broadcasted_iota(jnp.int32, (tm, 1), 0)
    ms = jnp.mean(x * x, axis=-1, keepdims=True)
    y = x * lax.rsqrt(ms + EPS) * g_ref[...]
    z = y * (1.0 + _stream_vec(mod_ref, shift_idx + 1, rows, n_ctx)) + _stream_vec(mod_ref, shift_idx, rows, n_ctx)
    z_ref[...] = z.astype(BF16)
    if with_router:
        _store_slabs(zp_ref, _pack_rows(z))
        lg_ref[...] = jnp.dot(z, rw_ref[...], preferred_element_type=F32, precision=lax.Precision.HIGHEST)


def norm_modulate(h, g, mod, n_ctx, shift_idx, router_w=None):
    t, d = h.shape
    tm = ROW_TILE
    with_router = router_w is not None
    in_specs = [
        pl.BlockSpec((tm, d), lambda i: (i, 0)),
        pl.BlockSpec((1, d), lambda i: (0, 0)),
        pl.BlockSpec((2, 6, d), lambda i: (0, 0, 0)),
    ]
    args = [h, g.reshape(1, d), mod]
    out_specs = [pl.BlockSpec((tm, d), lambda i: (i, 0))]
    out_shape = [jax.ShapeDtypeStruct((t, d), BF16)]
    if with_router:
        e = router_w.shape[1]
        rw = jnp.zeros((d, LANES), F32).at[:, :e].set(router_w)
        in_specs.append(pl.BlockSpec((d, LANES), lambda i: (0, 0)))
        args.append(rw)
        ns = d // 2 // LANES
        out_specs += [pl.BlockSpec((tm, ns, LANES), lambda i: (i, 0, 0)), pl.BlockSpec((tm, LANES), lambda i: (i, 0))]
        out_shape += [jax.ShapeDtypeStruct((t, ns, LANES), jnp.uint32), jax.ShapeDtypeStruct((t, LANES), F32)]
    outs = pl.pallas_call(
        functools.partial(_norm_mod_kernel, n_ctx=n_ctx, tm=tm, shift_idx=shift_idx, with_router=with_router),
        grid=(t // tm,),
        in_specs=in_specs,
        out_specs=out_specs,
        out_shape=out_shape,
        compiler_params=_cparams(("arbitrary",), 8 * tm * d * 4 + 4 * d * LANES * 4),
        name="norm_mod_router" if with_router else "norm_mod",
    )(*args)
    return outs if with_router else outs[0]


def _mm_kernel(x_ref, w_ref, o_ref, wb_sc):
    @pl.when(pl.program_id(1) == 0)
    def _():
        wb_sc[...] = w_ref[...].astype(BF16)

    o_ref[...] = jnp.dot(x_ref[...], wb_sc[...], preferred_element_type=F32).astype(o_ref.dtype)


def matmul(x, w, n, out_dtype):
    m, k = x.shape
    tm = _pick_tile(m, (1056, 1024, 768, 640, 512, 256))
    tn = _pick_tile(n, (512, 256, 128))
    osz = jnp.dtype(out_dtype).itemsize
    wsz = jnp.dtype(w.dtype).itemsize
    return pl.pallas_call(
        _mm_kernel,
        grid=(n // tn, m // tm),
        in_specs=[pl.BlockSpec((tm, k), lambda j, i: (i, 0)), pl.BlockSpec((k, tn), lambda j, i: (0, j))],
        out_specs=pl.BlockSpec((tm, tn), lambda j, i: (i, j)),
        out_shape=jax.ShapeDtypeStruct((m, n), out_dtype),
        scratch_shapes=[pltpu.VMEM((k, tn), BF16)],
        compiler_params=_cparams(("arbitrary", "arbitrary"),
                                 2 * (tm * k * 2 + k * tn * wsz + tm * tn * osz) + k * tn * 2 + tm * tn * 4),
        name="matmul",
    )(x, w)


LOG2_E = 1.4426950408889634


def _qk_prep_kernel(y_ref, cos_ref, sin_ref, qg_ref, kg_ref, qt_ref, k_ref, vt_ref, *, n_q, n_k):
    dh = ATTN_HEAD_DIM
    tq = y_ref.shape[0]
    cos = cos_ref[...]
    sin = sin_ref[...]
    lane = lax.broadcasted_iota(jnp.int32, cos.shape, 1)
    first_half = (lane % (dh // 2)) < (dh // 4)
    q_scale = dh ** -0.5 * LOG2_E
    for hd in range(n_q + n_k):
        x = y_ref[:, hd * dh:(hd + 1) * dh].astype(F32)
        g = qg_ref[...] if hd < n_q else kg_ref[...]
        y = x * lax.rsqrt(jnp.mean(x * x, axis=-1, keepdims=True) + EPS) * g
        partner = jnp.where(first_half, pltpu.roll(y, dh - dh // 4, axis=1), pltpu.roll(y, dh // 4, axis=1))
        r = y * cos + partner * sin
        if hd < n_q:
            kv, g_in = divmod(hd, GQA_GROUP)
            qt_ref[kv, 0, :, g_in * tq:(g_in + 1) * tq] = (r * q_scale).T.astype(BF16)
        else:
            k_ref[:, (hd - n_q) * dh:(hd - n_q + 1) * dh] = r.astype(BF16)
    for kv in range(n_k):
        v = y_ref[:, (n_q + n_k + kv) * dh:(n_q + n_k + kv + 1) * dh].astype(F32)
        vt_ref[kv, 0] = v.T.astype(BF16)


def qk_prepare(y, cos_t, sin_t, q_norm, k_norm, n_q, n_k):
    t = y.shape[0]
    dh = ATTN_HEAD_DIM
    w = (n_q + 2 * n_k) * dh
    tm = ROW_TILE
    return pl.pallas_call(
        functools.partial(_qk_prep_kernel, n_q=n_q, n_k=n_k),
        grid=(t // tm,),
        in_specs=[
            pl.BlockSpec((tm, w), lambda i: (i, 0)),
            pl.BlockSpec((tm, dh), lambda i: (i, 0)),
            pl.BlockSpec((tm, dh), lambda i: (i, 0)),
            pl.BlockSpec((1, dh), lambda i: (0, 0)),
            pl.BlockSpec((1, dh), lambda i: (0, 0)),
        ],
        out_specs=[
            pl.BlockSpec((n_k, 1, dh, GQA_GROUP * tm), lambda i: (0, i, 0, 0)),
            pl.BlockSpec((tm, n_k * dh), lambda i: (i, 0)),
            pl.BlockSpec((n_k, 1, dh, tm), lambda i: (0, i, 0, 0)),
        ],
        out_shape=[
            jax.ShapeDtypeStruct((n_k, t // tm, dh, GQA_GROUP * tm), BF16),
            jax.ShapeDtypeStruct((t, n_k * dh), BF16),
            jax.ShapeDtypeStruct((n_k, t // tm, dh, tm), BF16),
        ],
        compiler_params=_cparams(("arbitrary",), 10 * tm * w * 4),
        name="qk_prep",
    )(y, cos_t, sin_t, q_norm.reshape(1, dh), k_norm.reshape(1, dh))


def rope_tables(n_ctx, seq):
    nf = ATTN_HEAD_DIM // 4
    rows = seq // GRID_W
    row = jnp.repeat(jnp.arange(rows, dtype=F32), GRID_W)
    col = jnp.tile(jnp.arange(GRID_W, dtype=F32), rows)
    inv_freq = ROPE_THETA ** (-jnp.arange(nf, dtype=F32) / nf)
    ang = jnp.stack([row, col], axis=-1)[..., None] * inv_freq
    cos, sin = jnp.cos(ang), jnp.sin(ang)
    cos_l = jnp.concatenate([cos, cos], axis=-1).reshape(seq, 4 * nf)
    sin_l = jnp.concatenate([-sin, sin], axis=-1).reshape(seq, 4 * nf)
    cos_t = jnp.concatenate([jnp.ones((n_ctx, 4 * nf), F32), cos_l], axis=0)
    sin_t = jnp.concatenate([jnp.zeros((n_ctx, 4 * nf), F32), sin_l], axis=0)
    return cos_t, sin_t


def _attn_kernel(qt_ref, k_ref, vt_ref, o_ref, m_sc, l_sc, acc_sc, sa_sc, sb_sc, *, n_ctx, kc, n_lat_chunks):
    dh = ATTN_HEAD_DIM
    tq = o_ref.shape[0]
    qt = qt_ref[0, 0]

    s = jnp.dot(k_ref[0:n_ctx, :], qt, preferred_element_type=F32)
    m0 = jnp.max(s, axis=0, keepdims=True)
    p = jnp.exp2(s - m0)
    m_sc[...] = m0
    l_sc[...] = jnp.sum(p, axis=0, keepdims=True)
    acc_sc[...] = jnp.dot(vt_ref[0, 0], p.astype(BF16), preferred_element_type=F32)
    tiles = kc // n_ctx

    def scores(c):
        start = pl.multiple_of(n_ctx + jnp.minimum(c, n_lat_chunks - 1) * kc, n_ctx)
        return jnp.dot(k_ref[pl.ds(start, kc), :], qt, preferred_element_type=F32)

    def update(s, c):
        m_prev = m_sc[...]
        m_new = jnp.maximum(m_prev, jnp.max(s, axis=0, keepdims=True))
        alpha = jnp.exp2(m_prev - m_new)
        p = jnp.exp2(s - m_new)
        l_sc[...] = alpha * l_sc[...] + jnp.sum(p, axis=0, keepdims=True)
        vt = jnp.concatenate([vt_ref[0, 1 + c * tiles + j] for j in range(tiles)], axis=1)
        pv = jnp.dot(vt, p.astype(BF16), preferred_element_type=F32)
        acc_sc[...] = alpha * acc_sc[...] + pv
        m_sc[...] = m_new

    sa_sc[...] = scores(0)

    def body(i, carry):
        c = 2 * i
        sb_sc[...] = scores(c + 1)
        update(sa_sc[...], c)
        sa_sc[...] = scores(c + 2)
        update(sb_sc[...], c + 1)
        return carry

    lax.fori_loop(0, jnp.where(pl.program_id(1) == 0, 0, n_lat_chunks // 2), body, 0)
    out = acc_sc[...] / l_sc[...]
    for g in range(GQA_GROUP):
        o_ref[:, g * dh:(g + 1) * dh] = out[:, g * tq:(g + 1) * tq].T.astype(BF16)


def attention(qt, k, vt, n_ctx):
    n_kv, nblk, dh, cols = qt.shape
    t = k.shape[0]
    tq = cols // GQA_GROUP
    assert tq == n_ctx
    kc = _pick_tile(t - n_ctx, (2 * ATTN_KV_CHUNK, 2 * n_ctx)) // 2
    return pl.pallas_call(
        functools.partial(_attn_kernel, n_ctx=n_ctx, kc=kc, n_lat_chunks=(t - n_ctx) // kc),
        grid=(n_kv, nblk),
        in_specs=[
            pl.BlockSpec((1, 1, dh, cols), lambda h, i: (h, i, 0, 0)),
            pl.BlockSpec((t, dh), lambda h, i: (0, h)),
            pl.BlockSpec((1, nblk, dh, tq), lambda h, i: (h, 0, 0, 0)),
        ],
        out_specs=pl.BlockSpec((tq, GQA_GROUP * dh), lambda h, i: (i, h)),
        out_shape=jax.ShapeDtypeStruct((t, n_kv * GQA_GROUP * dh), BF16),
        scratch_shapes=[pltpu.VMEM((1, cols), F32), pltpu.VMEM((1, cols), F32), pltpu.VMEM((dh, cols), F32),
                        pltpu.VMEM((kc, cols), F32), pltpu.VMEM((kc, cols), F32)],
        compiler_params=_cparams(("arbitrary", "arbitrary"), 4 * t * dh * 2 + 8 * kc * cols * 4 + 4 * dh * cols * 4),
        name="attention",
    )(qt, k, vt)


def _mlstm_kernel(qf, kf, vf, gf, qb, kb, vb, gb, b_ref, hf_ref, hb_ref, ctf_sc, mf_sc, ctb_sc, mb_sc,
                  *, n_heads, dk, dv):
    @pl.when(pl.program_id(1) == 0)
    def _():
        for sc in (ctf_sc, mf_sc, ctb_sc, mb_sc):
            sc[...] = jnp.zeros_like(sc)

    kw = dict(n_heads=n_heads, dk=dk, dv=dv)
    _mlstm_chunk(qf, kf, vf, gf, b_ref, hf_ref, ctf_sc, mf_sc, direction=0, **kw)
    _mlstm_chunk(qb, kb, vb, gb, b_ref, hb_ref, ctb_sc, mb_sc, direction=1, **kw)


def _mlstm_chunk(q_ref, k_ref, v_ref, g_ref, b_ref, h_ref, ct_sc, m_sc, *, direction, n_heads, dk, dv):
    ln = q_ref.shape[0]
    head = pl.program_id(0)
    col_i = direction * 2 * n_heads + head
    col_f = col_i + n_heads

    pre = g_ref[...] + b_ref[...]
    pre = GATE_SOFTCAP * jnp.tanh(pre / GATE_SOFTCAP)
    lane = lax.broadcasted_iota(jnp.int32, pre.shape, 1)
    is_forget = ((lane // n_heads) % 2) == 1
    gates = jnp.where(is_forget, jax.nn.log_sigmoid(pre), pre)

    r = lax.broadcasted_iota(jnp.int32, (ln, ln), 0)
    c = lax.broadcasted_iota(jnp.int32, (ln, ln), 1)
    allowed = (r >= c) if direction == 0 else (r <= c)
    cum = jnp.dot(allowed.astype(F32), gates, preferred_element_type=F32, precision=lax.Precision.HIGHEST)

    def pick_col(a, idx):
        return jnp.sum(jnp.where(lane == idx, a, 0.0), axis=1, keepdims=True)

    sub = lax.broadcasted_iota(jnp.int32, (LANES, ln), 0)

    def pick_row(a, idx):
        return jnp.sum(jnp.where(sub == idx, a.T, 0.0), axis=0, keepdims=True)

    b_col = pick_col(cum, col_f)
    i_col = pick_col(gates, col_i)
    b_row = pick_row(cum, col_f)
    i_row = pick_row(gates, col_i)
    m_prev = m_sc[...]

    log_d = jnp.where(allowed, b_col - b_row + i_row, -jnp.inf)
    log_inter = b_col + m_prev
    m_t = jnp.maximum(log_inter, jnp.max(log_d, axis=1, keepdims=True))
    scale = dk ** -0.5
    d_mat = jnp.exp(log_d - m_t) * scale
    w_inter = jnp.exp(log_inter - m_t) * scale

    q = q_ref[...]
    k = k_ref[...]
    v_aug = jnp.concatenate([v_ref[...], jnp.ones((ln, LANES), BF16)], axis=1)
    s = lax.dot_general(q, k, (((1,), (1,)), ((), ())), preferred_element_type=F32) * d_mat
    intra = jnp.dot(s.astype(BF16), v_aug, preferred_element_type=F32)
    ct = ct_sc[...]
    inter = jnp.dot(q, ct.astype(BF16), preferred_element_type=F32)
    nd = intra + w_inter * inter
    den = nd[:, dv:dv + 1]
    h_ref[...] = nd[:, :dv] / jnp.maximum(jnp.abs(den), jnp.exp(-m_t))

    g_tot = jnp.sum(pick_col(gates, col_f), axis=0, keepdims=True)
    log_w = g_tot - b_col + i_col
    m_new = jnp.maximum(g_tot + m_prev, jnp.max(log_w, axis=0, keepdims=True))
    w = jnp.exp(log_w - m_new)
    decay = jnp.exp(g_tot + m_prev - m_new)
    wv = (w * v_aug.astype(F32)).astype(BF16)
    upd = lax.dot_general(k, wv, (((0,), (0,)), ((), ())), preferred_element_type=F32)
    ct_sc[...] = decay * ct + upd
    m_sc[...] = m_new


def mlstm(y, gates_raw, gate_bias, q_col0, k_col0, v_col0, dk, dv):
    t = y.shape[0]
    nh = MLSTM_HEADS
    ln = MLSTM_CHUNK
    nblk = t // ln
    bias = jnp.zeros((1, LANES), F32).at[0, :4 * nh].set(gate_bias)

    def fwd(j):
        return j

    def bwd(j):
        return jnp.where(j == 0, 0, nblk - j)

    def in_specs(blk):
        return [
            pl.BlockSpec((ln, dk), lambda h, j: (blk(j), q_col0 // dk + h)),
            pl.BlockSpec((ln, dk), lambda h, j: (blk(j), k_col0 // dk + h)),
            pl.BlockSpec((ln, dv), lambda h, j: (blk(j), v_col0 // dv + h)),
            pl.BlockSpec((ln, LANES), lambda h, j: (blk(j), 0)),
        ]

    state = [pltpu.VMEM((dk, dv + LANES), F32), pltpu.VMEM((1, 1), F32)]
    return pl.pallas_call(
        functools.partial(_mlstm_kernel, n_heads=nh, dk=dk, dv=dv),
        grid=(nh, nblk),
        in_specs=in_specs(fwd) + in_specs(bwd) + [pl.BlockSpec((1, LANES), lambda h, j: (0, 0))],
        out_specs=[pl.BlockSpec((ln, dv), lambda h, j: (fwd(j), h)), pl.BlockSpec((ln, dv), lambda h, j: (bwd(j), h))],
        out_shape=[jax.ShapeDtypeStruct((t, nh * dv), F32), jax.ShapeDtypeStruct((t, nh * dv), F32)],
        scratch_shapes=state + state,
        compiler_params=_cparams(("arbitrary", "arbitrary"), 40 * 2**20),
        name="mlstm_scan",
    )(y, y, y, gates_raw, y, y, y, gates_raw, bias)


def _mlstm_out_kernel(hf_ref, hb_ref, mo_ref, g_ref, o_ref):
    h = hf_ref[...] + hb_ref[...]
    hn = h * lax.rsqrt(jnp.mean(h * h, axis=-1, keepdims=True) + EPS) * g_ref[...]
    o_ref[...] = (hn * jax.nn.sigmoid(mo_ref[...].astype(F32))).astype(BF16)


def mlstm_output(hf, hb, y, mlstm_norm, mo_col0, dv):
    t = y.shape[0]
    nh = MLSTM_HEADS
    tm = ROW_TILE
    return pl.pallas_call(
        _mlstm_out_kernel,
        grid=(t // tm, nh),
        in_specs=[
            pl.BlockSpec((tm, dv), lambda i, h: (i, h)),
            pl.BlockSpec((tm, dv), lambda i, h: (i, h)),
            pl.BlockSpec((tm, dv), lambda i, h: (i, mo_col0 // dv + h)),
            pl.BlockSpec((1, dv), lambda i, h: (0, h)),
        ],
        out_specs=pl.BlockSpec((tm, dv), lambda i, h: (i, h)),
        out_shape=jax.ShapeDtypeStruct((t, nh * dv), BF16),
        compiler_params=_cparams(("arbitrary", "arbitrary"), 16 * tm * dv * 4),
        name="mlstm_out",
    )(hf, hb, y, mlstm_norm.reshape(1, nh * dv))


def _out_proj_kernel(a_ref, m_ref, w1_ref, w2_ref, h_ref, mod_ref, o_ref, w1_sc, w2_sc, *, n_ctx, tm, gate_idx):
    @pl.when(pl.program_id(1) == 0)
    def _():
        w1_sc[...] = w1_ref[...].astype(BF16)
        w2_sc[...] = w2_ref[...].astype(BF16)

    acc = jnp.dot(a_ref[...], w1_sc[...], preferred_element_type=F32)
    acc = acc + jnp.dot(m_ref[...], w2_sc[...], preferred_element_type=F32)
    rows = pl.program_id(1) * tm + lax.broadcasted_iota(jnp.int32, (tm, 1), 0)
    o_ref[...] = h_ref[...] + jnp.tanh(_stream_vec(mod_ref, gate_idx, rows, n_ctx)) * acc


def out_projection(a, m, w_out, h, mod, n_ctx, gate_idx):
    t, half = a.shape
    d = w_out.shape[1]
    tm = _pick_tile(t, (768, 512, 256))
    tn = _pick_tile(d, (512, 256, 128))
    return pl.pallas_call(
        functools.partial(_out_proj_kernel, n_ctx=n_ctx, tm=tm, gate_idx=gate_idx),
        grid=(d // tn, t // tm),
        in_specs=[
            pl.BlockSpec((tm, half), lambda j, i: (i, 0)),
            pl.BlockSpec((tm, half), lambda j, i: (i, 0)),
            pl.BlockSpec((half, tn), lambda j, i: (0, j)),
            pl.BlockSpec((half, tn), lambda j, i: (1, j)),
            pl.BlockSpec((tm, tn), lambda j, i: (i, j)),
            pl.BlockSpec((2, 6, tn), lambda j, i: (0, 0, j)),
        ],
        out_specs=pl.BlockSpec((tm, tn), lambda j, i: (i, j)),
        out_shape=jax.ShapeDtypeStruct((t, d), F32),
        scratch_shapes=[pltpu.VMEM((half, tn), BF16), pltpu.VMEM((half, tn), BF16)],
        compiler_params=_cparams(("arbitrary", "arbitrary"),
                                 2 * (2 * tm * half * 2 + 2 * half * tn * 4 + 2 * tm * tn * 4) + 2 * half * tn * 2
                                 + tm * tn * 4),
        name="out_proj",
    )(a, m, w_out, w_out, h, mod)


def _pool_kernel(h_ref, hp_ref, hn_ref, g_ref, mod_ref, o_ref, z_sc, *, seq, tm):
    i = pl.program_id(0)
    nblk = pl.num_programs(0)
    halo = POOL_HALO
    d = h_ref.shape[1]
    gd = d // len(POOL_WINDOWS)

    def normed(x):
        y = x * lax.rsqrt(jnp.mean(x * x, axis=-1, keepdims=True) + EPS) * g_ref[...]
        return y * (1.0 + mod_ref[0, 1:2, :]) + mod_ref[0, 0:1, :]

    z_sc[0:halo, :] = jnp.where(i > 0, normed(hp_ref[...]), 0.0)
    z_sc[halo:halo + tm, :] = normed(h_ref[...])
    z_sc[halo + tm:2 * halo + tm, :] = jnp.where(i < nblk - 1, normed(hn_ref[...]), 0.0)

    t = i * tm + lax.broadcasted_iota(jnp.int32, (tm, 1), 0)
    for gi, w in enumerate(POOL_WINDOWS):
        left = w // 2
        right = w - 1 - left
        cols = slice(gi * gd, (gi + 1) * gd)
        acc = z_sc[halo - left:halo - left + tm, cols]
        for off in range(-left + 1, right + 1):
            acc = acc + z_sc[halo + off:halo + off + tm, cols]
        cnt = (jnp.minimum(t + right + 1, seq) - jnp.maximum(t - left, 0)).astype(F32)
        o_ref[:, cols] = (acc / cnt - z_sc[halo:halo + tm, cols]).astype(BF16)


def pool_features(h, g, mod):
    seq, d = h.shape
    tm = ROW_TILE
    halo = POOL_HALO
    r = tm // halo
    nhb = seq // halo
    return pl.pallas_call(
        functools.partial(_pool_kernel, seq=seq, tm=tm),
        grid=(seq // tm,),
        in_specs=[
            pl.BlockSpec((tm, d), lambda i: (i, 0)),
            pl.BlockSpec((halo, d), lambda i: (jnp.maximum(i * r - 1, 0), 0)),
            pl.BlockSpec((halo, d), lambda i: (jnp.minimum((i + 1) * r, nhb - 1), 0)),
            pl.BlockSpec((1, d), lambda i: (0, 0)),
            pl.BlockSpec((2, 6, d), lambda i: (0, 0, 0)),
        ],
        out_specs=pl.BlockSpec((tm, d), lambda i: (i, 0)),
        out_shape=jax.ShapeDtypeStruct((seq, d), BF16),
        scratch_shapes=[pltpu.VMEM((tm + 2 * halo, d), F32)],
        compiler_params=_cparams(("arbitrary",), 10 * tm * d * 4),
        name="pool_features",
    )(h, h, h, g.reshape(1, d), mod)


def _pool_proj_kernel(x_ref, w_ref, ps_ref, h_ref, mod_ref, o_ref):
    acc = jnp.dot(x_ref[...], w_ref[0], preferred_element_type=F32)
    o_ref[...] = h_ref[...] + jnp.tanh(mod_ref[0, 2:3, :]) * (acc * ps_ref[...])


def pool_projection(dp, pool_w, pool_scale, h, mod):
    seq, d = h.shape
    ng, gd, _ = pool_w.shape
    tm = _pick_tile(seq, (1024, 512, 256))
    return pl.pallas_call(
        _pool_proj_kernel,
        grid=(ng, seq // tm),
        in_specs=[
            pl.BlockSpec((tm, gd), lambda g, i: (i, g)),
            pl.BlockSpec((1, gd, gd), lambda g, i: (g, 0, 0)),
            pl.BlockSpec((1, gd), lambda g, i: (0, g)),
            pl.BlockSpec((tm, gd), lambda g, i: (i, g)),
            pl.BlockSpec((2, 6, gd), lambda g, i: (0, 0, g)),
        ],
        out_specs=pl.BlockSpec((tm, gd), lambda g, i: (i, g)),
        out_shape=jax.ShapeDtypeStruct((seq, d), F32),
        compiler_params=_cparams(("arbitrary", "arbitrary"), 2 * (tm * gd * 10 + gd * gd * 2) + tm * gd * 4),
        name="pool_proj",
    )(dp, pool_w, pool_scale.reshape(1, d), h, mod)


def _route_kernel(lg_ref, bias_ref, gate_ref, idx_ref, rank_ref, cnt_ref, run_sc, *, n_exp):
    ng = N_GROUPS
    ge = n_exp // ng
    tt = lg_ref.shape[1]
    neg = -jnp.inf

    @pl.when(pl.program_id(0) == 0)
    def _():
        run_sc[...] = jnp.zeros_like(run_sc)

    scores = jax.nn.sigmoid(lg_ref[...])
    sel = (scores + bias_ref[...]).reshape(ng, ge, tt)
    e_in_g = lax.broadcasted_iota(jnp.int32, (ng, ge, tt), 1)
    m1 = jnp.max(sel, axis=1, keepdims=True)
    first = jnp.min(jnp.where(sel == m1, e_in_g, ge), axis=1, keepdims=True)
    m2 = jnp.max(jnp.where(e_in_g == first, neg, sel), axis=1, keepdims=True)
    gscore = m1 + m2

    gid = lax.broadcasted_iota(jnp.int32, (ng, 1, tt), 0)
    gmask = jnp.zeros((ng, 1, tt), jnp.bool_)
    for _ in range(TOPK_GROUPS):
        mx = jnp.max(gscore, axis=0, keepdims=True)
        pick = gid == jnp.min(jnp.where(gscore == mx, gid, ng), axis=0, keepdims=True)
        gmask = jnp.logical_or(gmask, pick)
        gscore = jnp.where(pick, neg, gscore)

    eid = lax.broadcasted_iota(jnp.int32, (ng, ge, tt), 0) * ge + e_in_g
    cand = jnp.where(gmask, sel, neg)
    chosen = jnp.zeros((ng, ge, tt), jnp.bool_)
    for kk in range(TOP_K):
        mx = jnp.max(jnp.max(cand, axis=1, keepdims=True), axis=0, keepdims=True)
        hit = jnp.where(cand == mx, eid, n_exp)
        pick_id = jnp.min(jnp.min(hit, axis=1, keepdims=True), axis=0, keepdims=True)
        pick = eid == pick_id
        chosen = jnp.logical_or(chosen, pick)
        cand = jnp.where(pick, neg, cand)
        idx_ref[kk:kk + 1, :] = pick_id.reshape(1, tt)

    s3 = scores.reshape(ng, ge, tt)
    w = jnp.where(chosen, s3, 0.0)
    denom = jnp.sum(jnp.sum(w, axis=1, keepdims=True), axis=0, keepdims=True)
    gate_ref[...] = (w / denom * ROUTED_SCALE).reshape(n_exp, tt)

    chosen_f = chosen.reshape(n_exp, tt).astype(F32)
    earlier = (lax.broadcasted_iota(jnp.int32, (tt, tt), 0) < lax.broadcasted_iota(jnp.int32, (tt, tt), 1))
    local = jnp.dot(chosen_f.astype(BF16), earlier.astype(BF16), preferred_element_type=F32)
    run = run_sc[...]
    rank_ref[...] = jnp.where(chosen.reshape(n_exp, tt), local + run, -1.0).astype(jnp.int32)
    run = run + jnp.sum(chosen_f, axis=1, keepdims=True)
    run_sc[...] = run
    cnt_ref[...] = jnp.broadcast_to(run, cnt_ref.shape).astype(jnp.int32)


def route(logits_t, router_bias):
    n_exp, t = logits_t.shape
    tt = _pick_tile(t, (512, 256, 128))
    gates, idx, rank, cnt = pl.pallas_call(
        functools.partial(_route_kernel, n_exp=n_exp),
        grid=(t // tt,),
        in_specs=[pl.BlockSpec((n_exp, tt), lambda i: (0, i)), pl.BlockSpec((n_exp, 1), lambda i: (0, 0))],
        out_specs=[
            pl.BlockSpec((n_exp, tt), lambda i: (0, i)),
            pl.BlockSpec((TOP_K, tt), lambda i: (0, i)),
            pl.BlockSpec((n_exp, tt), lambda i: (0, i)),
            pl.BlockSpec((n_exp, LANES), lambda i: (0, 0)),
        ],
        out_shape=[
            jax.ShapeDtypeStruct((n_exp, t), F32),
            jax.ShapeDtypeStruct((TOP_K, t), jnp.int32),
            jax.ShapeDtypeStruct((n_exp, t), jnp.int32),
            jax.ShapeDtypeStruct((n_exp, LANES), jnp.int32),
        ],
        scratch_shapes=[pltpu.VMEM((n_exp, 1), F32)],
        compiler_params=_cparams(("arbitrary",), 32 * 2**20),
        name="moe_route",
    )(logits_t, router_bias.reshape(n_exp, 1))
    return gates, idx, rank, cnt[:, 0]


def _slab_copy(src, src_row, dst, dst_row, sem):
    return pltpu.make_async_copy(src.at[src_row], dst.at[dst_row], sem)


def _dispatch_kernel(pos_ref, pad_ref, zp_ref, xs_ref, zero_sc, sem, pad_sem, *, tt, n_exp, tile, n_tiles):
    i = pl.program_id(0)

    def for_each_zero_copy(fn):
        def per_expert(e, carry):
            first = pad_ref[0, e]

            def one(r, c):
                fn(pltpu.make_async_copy(zero_sc.at[0], xs_ref.at[first + r], pad_sem))
                return c

            lax.fori_loop(0, pad_ref[1, e], one, 0)
            return carry

        lax.fori_loop(0, n_exp, per_expert, 0)

        def tail(tile_idx, c):
            fn(pltpu.make_async_copy(zero_sc, xs_ref.at[pl.ds(tile_idx * tile, tile)], pad_sem))
            return c

        lax.fori_loop(pad_ref[2, 0], n_tiles, tail, 0)

    @pl.when(i == 0)
    def _():
        zero_sc[...] = jnp.zeros_like(zero_sc)
        for_each_zero_copy(lambda cp: cp.start())

    def issue(j, carry):
        for k in range(TOP_K):
            _slab_copy(zp_ref, j, xs_ref, pos_ref[0, 0, j * TOP_K + k], sem).start()
        return carry

    lax.fori_loop(0, tt, issue, 0)
    for _ in range(tt * TOP_K):
        _slab_copy(zp_ref, 0, xs_ref, 0, sem).wait()

    @pl.when(i == pl.num_programs(0) - 1)
    def _():
        for_each_zero_copy(lambda cp: cp.wait())


def dispatch(zp, pos_tk, pad_info, n_tiles):
    t, ns, _ = zp.shape
    tt = COMBINE_TILE
    tile = EXPERT_TILE
    n_exp = pad_info.shape[1]
    return pl.pallas_call(
        functools.partial(_dispatch_kernel, tt=tt, n_exp=n_exp, tile=tile, n_tiles=n_tiles),
        grid=(t // tt,),
        in_specs=[
            pl.BlockSpec((1, 1, tt * TOP_K), lambda i: (i, 0, 0), memory_space=pltpu.SMEM),
            pl.BlockSpec(memory_space=pltpu.SMEM),
            pl.BlockSpec((tt, ns, LANES), lambda i: (i, 0, 0)),
        ],
        out_specs=pl.BlockSpec(memory_space=pl.ANY),
        out_shape=jax.ShapeDtypeStruct((n_tiles * tile, ns, LANES), jnp.uint32),
        scratch_shapes=[pltpu.VMEM((tile, ns, LANES), jnp.uint32), pltpu.SemaphoreType.DMA(()),
                        pltpu.SemaphoreType.DMA(())],
        compiler_params=_cparams(("arbitrary",), (2 * tt + tile) * ns * LANES * 4),
        name="moe_dispatch",
    )(pos_tk.reshape(t // tt, 1, tt * TOP_K), pad_info, zp)


def _expert_kernel(te_ref, first_ref, nused_ref, x_ref, wg_ref, wu_ref, wd_ref, y_ref, wg_sc, wu_sc, wd_sc):
    i = pl.program_id(0)

    @pl.when(i < nused_ref[0])
    def _():
        @pl.when(first_ref[i] == 1)
        def _():
            wg_sc[...] = wg_ref[0, 0].astype(BF16)
            wu_sc[...] = wu_ref[0, 0].astype(BF16)
            wd_sc[...] = wd_ref[0, 0].astype(BF16)

        x = _unpack_rows(_load_slabs(x_ref)).astype(BF16)
        g = jnp.dot(x, wg_sc[...], preferred_element_type=F32)
        u = jnp.dot(x, wu_sc[...], preferred_element_type=F32)
        hid = (g * jax.nn.sigmoid(g) * u).astype(BF16)
        y = jnp.dot(hid, wd_sc[...], preferred_element_type=F32)
        _store_slabs(y_ref, _pack_rows(y))

    @pl.when(i >= nused_ref[0])
    def _():
        y_ref[...] = jnp.zeros_like(y_ref)


def expert_ffn(xs, tile_expert, tile_first, n_used, layer, exp_gate, exp_up, exp_down):
    n_rows, ns, _ = xs.shape
    _, n_exp, d, f = exp_gate.shape
    tm = EXPERT_TILE
    n_tiles = n_rows // tm

    def row_blk(i, te, first, nused):
        return (jnp.minimum(i, nused[0] - 1), 0, 0)

    def w_blk(i, te, first, nused):
        return (layer, te[i], 0, 0)

    grid_spec = pltpu.PrefetchScalarGridSpec(
        num_scalar_prefetch=3,
        grid=(n_tiles,),
        in_specs=[
            pl.BlockSpec((tm, ns, LANES), row_blk),
            pl.BlockSpec((1, 1, d, f), w_blk),
            pl.BlockSpec((1, 1, d, f), w_blk),
            pl.BlockSpec((1, 1, f, d), w_blk),
        ],
        out_specs=pl.BlockSpec((tm, ns, LANES), lambda i, te, first, nused: (i, 0, 0)),
        scratch_shapes=[pltpu.VMEM((d, f), BF16), pltpu.VMEM((d, f), BF16), pltpu.VMEM((f, d), BF16)],
    )
    return pl.pallas_call(
        _expert_kernel,
        grid_spec=grid_spec,
        out_shape=jax.ShapeDtypeStruct((n_rows, ns, LANES), jnp.uint32),
        compiler_params=_cparams(("arbitrary",), 2 * 3 * d * f * 4 + 3 * d * f * 2 + 4 * tm * ns * LANES * 4
                                 + 6 * tm * d * 4),
        name="moe_experts",
    )(tile_expert, tile_first, n_used, xs, exp_gate, exp_up, exp_down)


def _combine_kernel(pos_ref, pos_next_ref, ys_ref, wt_ref, z_ref, sg_ref, su_ref, sd_ref, h_ref, mod_ref, o_ref,
                    buf, sem, *, tt, n_ctx, row0):
    i = pl.program_id(0)
    slot = i % 2

    def issue(p_ref, s):
        def body(j, carry):
            for k in range(TOP_K):
                _slab_copy(ys_ref, p_ref[0, 0, j * TOP_K + k], buf.at[s, k], j, sem.at[s]).start()
            return carry

        lax.fori_loop(0, tt, body, 0)

    @pl.when(i == 0)
    def _():
        issue(pos_ref, 0)

    @pl.when(i + 1 < pl.num_programs(0))
    def _():
        issue(pos_next_ref, 1 - slot)

    z = z_ref[...]
    g = jnp.dot(z, sg_ref[...], preferred_element_type=F32)
    u = jnp.dot(z, su_ref[...], preferred_element_type=F32)
    acc = jnp.dot((g * jax.nn.sigmoid(g) * u).astype(BF16), sd_ref[...], preferred_element_type=F32)

    for _ in range(tt * TOP_K):
        _slab_copy(ys_ref, 0, buf.at[slot, 0], 0, sem.at[slot]).wait()

    wt = wt_ref[...]
    for k in range(TOP_K):
        acc = acc + wt[:, k:k + 1] * _unpack_rows(_load_slabs(buf.at[slot, k]))
    rows = row0 + pl.program_id(0) * tt + lax.broadcasted_iota(jnp.int32, (tt, 1), 0)
    o_ref[...] = h_ref[...] + jnp.tanh(_stream_vec(mod_ref, 5, rows, n_ctx)) * acc


def combine(ys, pos_tk, w_tk, z, sh_gate, sh_up, sh_down, h, mod, n_ctx, row0):
    t, d = h.shape
    tt = COMBINE_TILE
    f = sh_gate.shape[1]
    ns = ys.shape[1]
    b0 = row0 // tt
    n_out = t - row0
    last = t // tt - 1
    pos_blocks = pos_tk.reshape(t // tt, 1, tt * TOP_K)
    return pl.pallas_call(
        functools.partial(_combine_kernel, tt=tt, n_ctx=n_ctx, row0=row0),
        grid=(n_out // tt,),
        in_specs=[
            pl.BlockSpec((1, 1, tt * TOP_K), lambda i: (b0 + i, 0, 0), memory_space=pltpu.SMEM),
            pl.BlockSpec((1, 1, tt * TOP_K), lambda i: (jnp.minimum(b0 + i + 1, last), 0, 0), memory_space=pltpu.SMEM),
            pl.BlockSpec(memory_space=pl.ANY),
            pl.BlockSpec((tt, TOP_K), lambda i: (b0 + i, 0)),
            pl.BlockSpec((tt, d), lambda i: (b0 + i, 0)),
            pl.BlockSpec((d, f), lambda i: (0, 0)),
            pl.BlockSpec((d, f), lambda i: (0, 0)),
            pl.BlockSpec((f, d), lambda i: (0, 0)),
            pl.BlockSpec((tt, d), lambda i: (b0 + i, 0)),
            pl.BlockSpec((2, 6, d), lambda i: (0, 0, 0)),
        ],
        out_specs=pl.BlockSpec((tt, d), lambda i: (i, 0)),
        out_shape=jax.ShapeDtypeStruct((n_out, d), F32),
        scratch_shapes=[pltpu.VMEM((2, TOP_K, tt, ns, LANES), jnp.uint32), pltpu.SemaphoreType.DMA((2,))],
        compiler_params=_cparams(("arbitrary",),
                                 2 * TOP_K * tt * ns * LANES * 4 + 12 * tt * d * 4 + 6 * d * f * 2 * 2),
        name="moe_combine",
    )(pos_blocks, pos_blocks, ys, w_tk, z, sh_gate, sh_up, sh_down, h, mod)


def moe_block(h, norm_g, mod, n_ctx, row0, layer, router_w, router_bias, exp_gate, exp_up, exp_down,
              sh_gate, sh_up, sh_down):
    t, d = h.shape
    n_exp = router_w.shape[1]
    z, zp, logits = norm_modulate(h, norm_g, mod, n_ctx, 3, router_w=router_w)
    gates, idx_t, rank, counts = route(logits[:, :n_exp].T, router_bias)

    tm = EXPERT_TILE
    padded = (counts + tm - 1) // tm * tm
    ends = jnp.cumsum(padded)
    starts = ends - padded
    n_tiles = (t * TOP_K) // tm + n_exp
    tile_start = jnp.arange(n_tiles, dtype=jnp.int32) * tm
    n_used = (ends[-1] // tm).astype(jnp.int32)
    tile_expert = jnp.sum((ends[None, :] <= tile_start[:, None]).astype(jnp.int32), axis=1)
    tile_expert = jnp.minimum(tile_expert, n_exp - 1)
    tile_expert = jnp.where(tile_start < ends[-1], tile_expert, tile_expert[jnp.maximum(n_used - 1, 0)])
    tile_first = jnp.concatenate([jnp.ones((1,), jnp.int32), (tile_expert[1:] != tile_expert[:-1]).astype(jnp.int32)])

    pos_dense = starts[:, None] + rank
    pos_tk = jnp.take_along_axis(pos_dense, idx_t, axis=0).T.astype(jnp.int32)
    w_tk = jnp.take_along_axis(gates, idx_t, axis=0).T

    pad_info = jnp.stack([starts + counts, padded - counts, jnp.full_like(counts, n_used)]).astype(jnp.int32)
    xs = dispatch(zp, pos_tk, pad_info, n_tiles)
    ys = expert_ffn(xs, tile_expert, tile_first, n_used.reshape(1), layer, exp_gate, exp_up, exp_down)
    return combine(ys, pos_tk, w_tk, z, sh_gate.astype(BF16), sh_up.astype(BF16), sh_down.astype(BF16),
                   h, mod, n_ctx, row0)


def mixer_layer(h, n_ctx, mod, norm_g, w_in, gate_bias, q_norm, k_norm, mlstm_norm, w_out, cos_t, sin_t):
    t, d = h.shape
    dh = ATTN_HEAD_DIM
    n_q = (d // 2) // dh
    n_kv = n_q // GQA_GROUP
    nh = MLSTM_HEADS
    dv = (d // 2) // nh
    dk = dv // 2
    sizes = (n_q * dh, n_kv * dh, n_kv * dh, nh * dk, nh * dk, nh * dv, nh * dv, 4 * nh)
    off = [0]
    for s in sizes:
        off.append(off[-1] + s)
    n_main = off[7]

    z = norm_modulate(h, norm_g, mod, n_ctx, 0)
    y = matmul(z, w_in, n_main, BF16)
    wg = jnp.zeros((d, LANES), F32).at[:, :4 * nh].set(w_in[:, n_main:])
    gates_raw = matmul(z, wg, LANES, F32)

    qt, k, vt = qk_prepare(y, cos_t, sin_t, q_norm, k_norm, n_q, n_kv)
    a = attention(qt, k, vt, n_ctx)
    hf, hb = mlstm(y, gates_raw, gate_bias, off[3], off[4], off[5], dk, dv)
    m = mlstm_output(hf, hb, y, mlstm_norm, off[6], dv)
    return out_projection(a, m, w_out, h, mod, n_ctx, 2)


def kernel(x, c, ctx, c_ctx, ada_w, ada_b, norm_mix, norm_ffn, w_in, gate_bias, q_norm, k_norm, mlstm_norm, w_out,
           pool_w, pool_scale, router_w, router_bias, exp_gate, exp_up, exp_down, sh_gate, sh_up, sh_down):
    depth = ada_w.shape[0]
    seq = x.shape[1]
    n_ctx = ctx.shape[1]
    assert x.shape[0] == 1 and n_ctx == ROW_TILE and seq % ROW_TILE == 0
    mods = ada_modulation(c, c_ctx, ada_w, ada_b)
    cos_t, sin_t = rope_tables(n_ctx, seq)

    h = jnp.concatenate([ctx[0], x[0]], axis=0)
    nc = n_ctx
    for layer in range(depth):
        last = layer == depth - 1
        j = layer // 2
        mod = mods[layer]
        if layer % 2 == 0:
            h = mixer_layer(h, nc, mod, norm_mix[layer], w_in[j], gate_bias[j], q_norm[j], k_norm[j],
                            mlstm_norm[j], w_out[j], cos_t, sin_t)
        else:
            assert nc == 0 or not last
            dp = pool_features(h, norm_mix[layer], mod)
            h = pool_projection(dp, pool_w[j].astype(BF16), pool_scale[j], h, mod)
        drop_ctx = nc > 0 and (last or (layer + 1 == depth - 1 and (depth - 1) % 2 == 1))
        row0 = nc if drop_ctx else 0
        h = moe_block(h, norm_ffn[layer], mod, nc, row0, layer, router_w[layer], router_bias[layer], exp_gate,
                      exp_up, exp_down, sh_gate[layer], sh_up[layer], sh_down[layer])
        if drop_ctx:
            nc = 0
    return h[nc:][None]
```

```python
import functools

import jax
import jax.numpy as jnp
from jax import lax
from jax.experimental import pallas as pl
from jax.experimental.pallas import tpu as pltpu

F32 = jnp.float32
BF16 = jnp.bfloat16

EPS = 1e-6
ATTN_HEAD_DIM = 128
GQA_GROUP = 4
ROPE_THETA = 10000.0
GRID_W = 64
MLSTM_HEADS = 4
GATE_SOFTCAP = 15.0
POOL_WINDOWS = (2, 4, 8, 16)
POOL_HALO = 16
TOP_K = 8
N_GROUPS = 8
TOPK_GROUPS = 4
ROUTED_SCALE = 2.5

LANES = 128
V7X_VMEM_BUDGET = 56 * 2**20
COMPILER_TEMP_BYTES = 8 * 2**20
ROW_TILE = 256
MLSTM_CHUNK = 256
ATTN_KV_CHUNK = 1024
EXPERT_TILE = 256
COMBINE_TILE = 128


def _cparams(sem, vmem_bytes):
    limit = min(vmem_bytes + COMPILER_TEMP_BYTES, V7X_VMEM_BUDGET)
    return pltpu.CompilerParams(dimension_semantics=sem, vmem_limit_bytes=int(limit))


def _pick_tile(n, candidates):
    for c in candidates:
        if n % c == 0:
            return c
    raise ValueError(f"no tile for {n}")


ADA_K_CHUNK = 512


def _ada_kernel(s_ref, w_ref, b_ref, o_ref):
    d = w_ref.shape[1]
    tn = w_ref.shape[2]
    sub = 8

    def body(kc, acc):
        k0 = pl.multiple_of(kc * ADA_K_CHUNK, ADA_K_CHUNK)
        s = s_ref[pl.ds(k0, ADA_K_CHUNK), :]
        s = (s * jax.nn.sigmoid(s)).reshape(ADA_K_CHUNK // sub, sub, LANES)
        w = w_ref[0, pl.ds(k0, ADA_K_CHUNK), :].reshape(ADA_K_CHUNK // sub, sub, tn)
        return tuple(a + jnp.sum(w * s[:, :, j:j + 1], axis=0) for j, a in enumerate(acc))

    acc = lax.fori_loop(0, d // ADA_K_CHUNK, body, (jnp.zeros((sub, tn), F32), jnp.zeros((sub, tn), F32)))
    for j, a in enumerate(acc):
        o_ref[0, j:j + 1, :] = jnp.sum(a, axis=0, keepdims=True) + b_ref[0]


def ada_modulation(c, c_ctx, ada_w, ada_b):
    depth, d, n = ada_w.shape
    s = jnp.zeros((d, LANES), F32).at[:, 0].set(c[0]).at[:, 1].set(c_ctx)
    tn = _pick_tile(n, (1024, 512, 256, 128))
    out = pl.pallas_call(
        _ada_kernel,
        grid=(depth, n // tn),
        in_specs=[
            pl.BlockSpec((d, LANES), lambda l, j: (0, 0)),
            pl.BlockSpec((1, d, tn), lambda l, j: (l, 0, j)),
            pl.BlockSpec((1, 1, tn), lambda l, j: (l, 0, j)),
        ],
        out_specs=pl.BlockSpec((1, 2, tn), lambda l, j: (l, 0, j)),
        out_shape=jax.ShapeDtypeStruct((depth, 2, n), F32),
        compiler_params=_cparams(("arbitrary", "arbitrary"), 2 * d * tn * 4 + 8 * ADA_K_CHUNK * tn * 4),
        name="ada_mod",
    )(s, ada_w, ada_b.reshape(depth, 1, n))
    return out.reshape(depth, 2, 6, d)


def _stream_vec(mod_ref, idx, rows, n_ctx):
    lat = mod_ref[0, idx:idx + 1, :]
    if n_ctx == 0:
        return lat
    return jnp.where(rows < n_ctx, mod_ref[1, idx:idx + 1, :], lat)


def _pack_rows(x):
    half = x.shape[1] // 2
    return pltpu.pack_elementwise([x[:, :half], x[:, half:]], packed_dtype=BF16)


def _unpack_rows(xp):
    lo = pltpu.unpack_elementwise(xp, index=0, packed_dtype=BF16, unpacked_dtype=F32)
    hi = pltpu.unpack_elementwise(xp, index=1, packed_dtype=BF16, unpacked_dtype=F32)
    return jnp.concatenate([lo, hi], axis=1)


SUBLANES = 8


def _sublane_transpose(v):
    ax = v[0].ndim - 2
    sub = lax.broadcasted_iota(jnp.int32, v[0].shape, ax)
    for b in (4, 2, 1):
        upper = (sub & b) != 0
        nxt = []
        for a in range(SUBLANES):
            other = v[a ^ b]
            if a & b == 0:
                nxt.append(jnp.where(upper, pltpu.roll(other, b, axis=ax), v[a]))
            else:
                nxt.append(jnp.where(upper, v[a], pltpu.roll(other, SUBLANES - b, axis=ax)))
        v = nxt
    return v


def _store_slabs(ref, xp):
    rows, ns, _ = ref.shape
    x = pltpu.bitcast(xp, jnp.int32)
    u = [x[:, c * LANES:(c + 1) * LANES].reshape(rows // SUBLANES, SUBLANES, LANES) for c in range(ns)]
    v = [jnp.stack([u[SUBLANES * h + s] for h in range(ns // SUBLANES)], axis=1) for s in range(SUBLANES)]
    w = _sublane_transpose(v)
    ref[...] = pltpu.bitcast(jnp.stack(w, axis=1).reshape(rows, ns, LANES), jnp.uint32)


def _load_slabs(ref):
    rows, ns, _ = ref.shape
    x = pltpu.bitcast(ref[...], jnp.int32).reshape(rows // SUBLANES, SUBLANES, ns // SUBLANES, SUBLANES, LANES)
    v = _sublane_transpose([x[:, j] for j in range(SUBLANES)])
    cols = [v[s][:, h].reshape(rows, LANES) for h in range(ns // SUBLANES) for s in range(SUBLANES)]
    return pltpu.bitcast(jnp.concatenate(cols, axis=1), jnp.uint32)


def _norm_mod_kernel(h_ref, g_ref, mod_ref, *rest, n_ctx, tm, shift_idx, with_router):
    if with_router:
        rw_ref, z_ref, zp_ref, lg_ref = rest
    else:
        (z_ref,) = rest
    x = h_ref[...]
    rows = pl.program_id(0) * tm + lax.broadcasted_iota(jnp.int32, (tm, 1), 0)
    ms = jnp.mean(x * x, axis=-1, keepdims=True)
    y = x * lax.rsqrt(ms + EPS) * g_ref[...]
    z = y * (1.0 + _stream_vec(mod_ref, shift_idx + 1, rows, n_ctx)) + _stream_vec(mod_ref, shift_idx, rows, n_ctx)
    z_ref[...] = z.astype(BF16)
    if with_router:
        _store_slabs(zp_ref, _pack_rows(z))
        lg_ref[...] = jnp.dot(z, rw_ref[...], preferred_element_type=F32, precision=lax.Precision.HIGHEST)


def norm_modulate(h, g, mod, n_ctx, shift_idx, router_w=None):
    t, d = h.shape
    tm = ROW_TILE
    with_router = router_w is not None
    in_specs = [
        pl.BlockSpec((tm, d), lambda i: (i, 0)),
        pl.BlockSpec((1, d), lambda i: (0, 0)),
        pl.BlockSpec((2, 6, d), lambda i: (0, 0, 0)),
    ]
    args = [h, g.reshape(1, d), mod]
    out_specs = [pl.BlockSpec((tm, d), lambda i: (i, 0))]
    out_shape = [jax.ShapeDtypeStruct((t, d), BF16)]
    if with_router:
        e = router_w.shape[1]
        rw = jnp.zeros((d, LANES), F32).at[:, :e].set(router_w)
        in_specs.append(pl.BlockSpec((d, LANES), lambda i: (0, 0)))
        args.append(rw)
        ns = d // 2 // LANES
        out_specs += [pl.BlockSpec((tm, ns, LANES), lambda i: (i, 0, 0)), pl.BlockSpec((tm, LANES), lambda i: (i, 0))]
        out_shape += [jax.ShapeDtypeStruct((t, ns, LANES), jnp.uint32), jax.ShapeDtypeStruct((t, LANES), F32)]
    outs = pl.pallas_call(
        functools.partial(_norm_mod_kernel, n_ctx=n_ctx, tm=tm, shift_idx=shift_idx, with_router=with_router),
        grid=(t // tm,),
        in_specs=in_specs,
        out_specs=out_specs,
        out_shape=out_shape,
        compiler_params=_cparams(("arbitrary",), 8 * tm * d * 4 + 4 * d * LANES * 4),
        name="norm_mod_router" if with_router else "norm_mod",
    )(*args)
    return outs if with_router else outs[0]


def _mm_kernel(x_ref, w_ref, o_ref, wb_sc):
    @pl.when(pl.program_id(1) == 0)
    def _():
        wb_sc[...] = w_ref[...].astype(BF16)

    o_ref[...] = jnp.dot(x_ref[...], wb_sc[...], preferred_element_type=F32).astype(o_ref.dtype)


def matmul(x, w, n, out_dtype):
    m, k = x.shape
    tm = _pick_tile(m, (1056, 1024, 768, 640, 512, 256))
    tn = _pick_tile(n, (512, 256, 128))
    osz = jnp.dtype(out_dtype).itemsize
    wsz = jnp.dtype(w.dtype).itemsize
    return pl.pallas_call(
        _mm_kernel,
        grid=(n // tn, m // tm),
        in_specs=[pl.BlockSpec((tm, k), lambda j, i: (i, 0)), pl.BlockSpec((k, tn), lambda j, i: (0, j))],
        out_specs=pl.BlockSpec((tm, tn), lambda j, i: (i, j)),
        out_shape=jax.ShapeDtypeStruct((m, n), out_dtype),
        scratch_shapes=[pltpu.VMEM((k, tn), BF16)],
        compiler_params=_cparams(("arbitrary", "arbitrary"),
                                 2 * (tm * k * 2 + k * tn * wsz + tm * tn * osz) + k * tn * 2 + tm * tn * 4),
        name="matmul",
    )(x, w)


LOG2_E = 1.4426950408889634


def _qk_prep_kernel(y_ref, cos_ref, sin_ref, qg_ref, kg_ref, qt_ref, k_ref, vt_ref, *, n_q, n_k):
    dh = ATTN_HEAD_DIM
    tq = y_ref.shape[0]
    cos = cos_ref[...]
    sin = sin_ref[...]
    lane = lax.broadcasted_iota(jnp.int32, cos.shape, 1)
    first_half = (lane % (dh // 2)) < (dh // 4)
    q_scale = dh ** -0.5 * LOG2_E
    for hd in range(n_q + n_k):
        x = y_ref[:, hd * dh:(hd + 1) * dh].astype(F32)
        g = qg_ref[...] if hd < n_q else kg_ref[...]
        y = x * lax.rsqrt(jnp.mean(x * x, axis=-1, keepdims=True) + EPS) * g
        partner = jnp.where(first_half, pltpu.roll(y, dh - dh // 4, axis=1), pltpu.roll(y, dh // 4, axis=1))
        r = y * cos + partner * sin
        if hd < n_q:
            kv, g_in = divmod(hd, GQA_GROUP)
            qt_ref[kv, 0, :, g_in * tq:(g_in + 1) * tq] = (r * q_scale).T.astype(BF16)
        else:
            k_ref[:, (hd - n_q) * dh:(hd - n_q + 1) * dh] = r.astype(BF16)
    for kv in range(n_k):
        v = y_ref[:, (n_q + n_k + kv) * dh:(n_q + n_k + kv + 1) * dh].astype(F32)
        vt_ref[kv, 0] = v.T.astype(BF16)


def qk_prepare(y, cos_t, sin_t, q_norm, k_norm, n_q, n_k):
    t = y.shape[0]
    dh = ATTN_HEAD_DIM
    w = (n_q + 2 * n_k) * dh
    tm = ROW_TILE
    return pl.pallas_call(
        functools.partial(_qk_prep_kernel, n_q=n_q, n_k=n_k),
        grid=(t // tm,),
        in_specs=[
            pl.BlockSpec((tm, w), lambda i: (i, 0)),
            pl.BlockSpec((tm, dh), lambda i: (i, 0)),
            pl.BlockSpec((tm, dh), lambda i: (i, 0)),
            pl.BlockSpec((1, dh), lambda i: (0, 0)),
            pl.BlockSpec((1, dh), lambda i: (0, 0)),
        ],
        out_specs=[
            pl.BlockSpec((n_k, 1, dh, GQA_GROUP * tm), lambda i: (0, i, 0, 0)),
            pl.BlockSpec((tm, n_k * dh), lambda i: (i, 0)),
            pl.BlockSpec((n_k, 1, dh, tm), lambda i: (0, i, 0, 0)),
        ],
        out_shape=[
            jax.ShapeDtypeStruct((n_k, t // tm, dh, GQA_GROUP * tm), BF16),
            jax.ShapeDtypeStruct((t, n_k * dh), BF16),
            jax.ShapeDtypeStruct((n_k, t // tm, dh, tm), BF16),
        ],
        compiler_params=_cparams(("arbitrary",), 10 * tm * w * 4),
        name="qk_prep",
    )(y, cos_t, sin_t, q_norm.reshape(1, dh), k_norm.reshape(1, dh))


def rope_tables(n_ctx, seq):
    nf = ATTN_HEAD_DIM // 4
    rows = seq // GRID_W
    row = jnp.repeat(jnp.arange(rows, dtype=F32), GRID_W)
    col = jnp.tile(jnp.arange(GRID_W, dtype=F32), rows)
    inv_freq = ROPE_THETA ** (-jnp.arange(nf, dtype=F32) / nf)
    ang = jnp.stack([row, col], axis=-1)[..., None] * inv_freq
    cos, sin = jnp.cos(ang), jnp.sin(ang)
    cos_l = jnp.concatenate([cos, cos], axis=-1).reshape(seq, 4 * nf)
    sin_l = jnp.concatenate([-sin, sin], axis=-1).reshape(seq, 4 * nf)
    cos_t = jnp.concatenate([jnp.ones((n_ctx, 4 * nf), F32), cos_l], axis=0)
    sin_t = jnp.concatenate([jnp.zeros((n_ctx, 4 * nf), F32), sin_l], axis=0)
    return cos_t, sin_t


def _attn_kernel(qt_ref, k_ref, vt_ref, o_ref, m_sc, l_sc, acc_sc, sa_sc, sb_sc, *, n_ctx, kc, n_lat_chunks):
    dh = ATTN_HEAD_DIM
    tq = o_ref.shape[0]
    qt = qt_ref[0, 0]

    s = jnp.dot(k_ref[0:n_ctx, :], qt, preferred_element_type=F32)
    m0 = jnp.max(s, axis=0, keepdims=True)
    p = jnp.exp2(s - m0)
    m_sc[...] = m0
    l_sc[...] = jnp.sum(p, axis=0, keepdims=True)
    acc_sc[...] = jnp.dot(vt_ref[0, 0], p.astype(BF16), preferred_element_type=F32)
    tiles = kc // n_ctx

    def scores(c):
        start = pl.multiple_of(n_ctx + jnp.minimum(c, n_lat_chunks - 1) * kc, n_ctx)
        return jnp.dot(k_ref[pl.ds(start, kc), :], qt, preferred_element_type=F32)

    def update(s, c):
        m_prev = m_sc[...]
        m_new = jnp.maximum(m_prev, jnp.max(s, axis=0, keepdims=True))
        alpha = jnp.exp2(m_prev - m_new)
        p = jnp.exp2(s - m_new)
        l_sc[...] = alpha * l_sc[...] + jnp.sum(p, axis=0, keepdims=True)
        vt = jnp.concatenate([vt_ref[0, 1 + c * tiles + j] for j in range(tiles)], axis=1)
        pv = jnp.dot(vt, p.astype(BF16), preferred_element_type=F32)
        acc_sc[...] = alpha * acc_sc[...] + pv
        m_sc[...] = m_new

    sa_sc[...] = scores(0)

    def body(i, carry):
        c = 2 * i
        sb_sc[...] = scores(c + 1)
        update(sa_sc[...], c)
        sa_sc[...] = scores(c + 2)
        update(sb_sc[...], c + 1)
        return carry

    lax.fori_loop(0, jnp.where(pl.program_id(1) == 0, 0, n_lat_chunks // 2), body, 0)
    out = acc_sc[...] / l_sc[...]
    for g in range(GQA_GROUP):
        o_ref[:, g * dh:(g + 1) * dh] = out[:, g * tq:(g + 1) * tq].T.astype(BF16)


def attention(qt, k, vt, n_ctx):
    n_kv, nblk, dh, cols = qt.shape
    t = k.shape[0]
    tq = cols // GQA_GROUP
    assert tq == n_ctx
    kc = _pick_tile(t - n_ctx, (2 * ATTN_KV_CHUNK, 2 * n_ctx)) // 2
    return pl.pallas_call(
        functools.partial(_attn_kernel, n_ctx=n_ctx, kc=kc, n_lat_chunks=(t - n_ctx) // kc),
        grid=(n_kv, nblk),
        in_specs=[
            pl.BlockSpec((1, 1, dh, cols), lambda h, i: (h, i, 0, 0)),
            pl.BlockSpec((t, dh), lambda h, i: (0, h)),
            pl.BlockSpec((1, nblk, dh, tq), lambda h, i: (h, 0, 0, 0)),
        ],
        out_specs=pl.BlockSpec((tq, GQA_GROUP * dh), lambda h, i: (i, h)),
        out_shape=jax.ShapeDtypeStruct((t, n_kv * GQA_GROUP * dh), BF16),
        scratch_shapes=[pltpu.VMEM((1, cols), F32), pltpu.VMEM((1, cols), F32), pltpu.VMEM((dh, cols), F32),
                        pltpu.VMEM((kc, cols), F32), pltpu.VMEM((kc, cols), F32)],
        compiler_params=_cparams(("arbitrary", "arbitrary"), 4 * t * dh * 2 + 8 * kc * cols * 4 + 4 * dh * cols * 4),
        name="attention",
    )(qt, k, vt)


def _mlstm_kernel(qf, kf, vf, gf, qb, kb, vb, gb, b_ref, hf_ref, hb_ref, ctf_sc, mf_sc, ctb_sc, mb_sc,
                  *, n_heads, dk, dv):
    @pl.when(pl.program_id(1) == 0)
    def _():
        for sc in (ctf_sc, mf_sc, ctb_sc, mb_sc):
            sc[...] = jnp.zeros_like(sc)

    kw = dict(n_heads=n_heads, dk=dk, dv=dv)
    _mlstm_chunk(qf, kf, vf, gf, b_ref, hf_ref, ctf_sc, mf_sc, direction=0, **kw)
    _mlstm_chunk(qb, kb, vb, gb, b_ref, hb_ref, ctb_sc, mb_sc, direction=1, **kw)


def _mlstm_chunk(q_ref, k_ref, v_ref, g_ref, b_ref, h_ref, ct_sc, m_sc, *, direction, n_heads, dk, dv):
    ln = q_ref.shape[0]
    head = pl.program_id(0)
    col_i = direction * 2 * n_heads + head
    col_f = col_i + n_heads

    pre = g_ref[...] + b_ref[...]
    pre = GATE_SOFTCAP * jnp.tanh(pre / GATE_SOFTCAP)
    lane = lax.broadcasted_iota(jnp.int32, pre.shape, 1)
    is_forget = ((lane // n_heads) % 2) == 1
    gates = jnp.where(is_forget, jax.nn.log_sigmoid(pre), pre)

    r = lax.broadcasted_iota(jnp.int32, (ln, ln), 0)
    c = lax.broadcasted_iota(jnp.int32, (ln, ln), 1)
    allowed = (r >= c) if direction == 0 else (r <= c)
    cum = jnp.dot(allowed.astype(F32), gates, preferred_element_type=F32, precision=lax.Precision.HIGHEST)

    def pick_col(a, idx):
        return jnp.sum(jnp.where(lane == idx, a, 0.0), axis=1, keepdims=True)

    sub = lax.broadcasted_iota(jnp.int32, (LANES, ln), 0)

    def pick_row(a, idx):
        return jnp.sum(jnp.where(sub == idx, a.T, 0.0), axis=0, keepdims=True)

    b_col = pick_col(cum, col_f)
    i_col = pick_col(gates, col_i)
    b_row = pick_row(cum, col_f)
    i_row = pick_row(gates, col_i)
    m_prev = m_sc[...]

    log_d = jnp.where(allowed, b_col - b_row + i_row, -jnp.inf)
    log_inter = b_col + m_prev
    m_t = jnp.maximum(log_inter, jnp.max(log_d, axis=1, keepdims=True))
    scale = dk ** -0.5
    d_mat = jnp.exp(log_d - m_t) * scale
    w_inter = jnp.exp(log_inter - m_t) * scale

    q = q_ref[...]
    k = k_ref[...]
    v_aug = jnp.concatenate([v_ref[...], jnp.ones((ln, LANES), BF16)], axis=1)
    s = lax.dot_general(q, k, (((1,), (1,)), ((), ())), preferred_element_type=F32) * d_mat
    intra = jnp.dot(s.astype(BF16), v_aug, preferred_element_type=F32)
    ct = ct_sc[...]
    inter = jnp.dot(q, ct.astype(BF16), preferred_element_type=F32)
    nd = intra + w_inter * inter
    den = nd[:, dv:dv + 1]
    h_ref[...] = nd[:, :dv] / jnp.maximum(jnp.abs(den), jnp.exp(-m_t))

    g_tot = jnp.sum(pick_col(gates, col_f), axis=0, keepdims=True)
    log_w = g_tot - b_col + i_col
    m_new = jnp.maximum(g_tot + m_prev, jnp.max(log_w, axis=0, keepdims=True))
    w = jnp.exp(log_w - m_new)
    decay = jnp.exp(g_tot + m_prev - m_new)
    wv = (w * v_aug.astype(F32)).astype(BF16)
    upd = lax.dot_general(k, wv, (((0,), (0,)), ((), ())), preferred_element_type=F32)
    ct_sc[...] = decay * ct + upd
    m_sc[...] = m_new


def mlstm(y, gates_raw, gate_bias, q_col0, k_col0, v_col0, dk, dv):
    t = y.shape[0]
    nh = MLSTM_HEADS
    ln = MLSTM_CHUNK
    nblk = t // ln
    bias = jnp.zeros((1, LANES), F32).at[0, :4 * nh].set(gate_bias)

    def fwd(j):
        return j

    def bwd(j):
        return jnp.where(j == 0, 0, nblk - j)

    def in_specs(blk):
        return [
            pl.BlockSpec((ln, dk), lambda h, j: (blk(j), q_col0 // dk + h)),
            pl.BlockSpec((ln, dk), lambda h, j: (blk(j), k_col0 // dk + h)),
            pl.BlockSpec((ln, dv), lambda h, j: (blk(j), v_col0 // dv + h)),
            pl.BlockSpec((ln, LANES), lambda h, j: (blk(j), 0)),
        ]

    state = [pltpu.VMEM((dk, dv + LANES), F32), pltpu.VMEM((1, 1), F32)]
    return pl.pallas_call(
        functools.partial(_mlstm_kernel, n_heads=nh, dk=dk, dv=dv),
        grid=(nh, nblk),
        in_specs=in_specs(fwd) + in_specs(bwd) + [pl.BlockSpec((1, LANES), lambda h, j: (0, 0))],
        out_specs=[pl.BlockSpec((ln, dv), lambda h, j: (fwd(j), h)), pl.BlockSpec((ln, dv), lambda h, j: (bwd(j), h))],
        out_shape=[jax.ShapeDtypeStruct((t, nh * dv), F32), jax.ShapeDtypeStruct((t, nh * dv), F32)],
        scratch_shapes=state + state,
        compiler_params=_cparams(("arbitrary", "arbitrary"), 40 * 2**20),
        name="mlstm_scan",
    )(y, y, y, gates_raw, y, y, y, gates_raw, bias)


def _mlstm_out_kernel(hf_ref, hb_ref, mo_ref, g_ref, o_ref):
    h = hf_ref[...] + hb_ref[...]
    hn = h * lax.rsqrt(jnp.mean(h * h, axis=-1, keepdims=True) + EPS) * g_ref[...]
    o_ref[...] = (hn * jax.nn.sigmoid(mo_ref[...].astype(F32))).astype(BF16)


def mlstm_output(hf, hb, y, mlstm_norm, mo_col0, dv):
    t = y.shape[0]
    nh = MLSTM_HEADS
    tm = ROW_TILE
    return pl.pallas_call(
        _mlstm_out_kernel,
        grid=(t // tm, nh),
        in_specs=[
            pl.BlockSpec((tm, dv), lambda i, h: (i, h)),
            pl.BlockSpec((tm, dv), lambda i, h: (i, h)),
            pl.BlockSpec((tm, dv), lambda i, h: (i, mo_col0 // dv + h)),
            pl.BlockSpec((1, dv), lambda i, h: (0, h)),
        ],
        out_specs=pl.BlockSpec((tm, dv), lambda i, h: (i, h)),
        out_shape=jax.ShapeDtypeStruct((t, nh * dv), BF16),
        compiler_params=_cparams(("arbitrary", "arbitrary"), 16 * tm * dv * 4),
        name="mlstm_out",
    )(hf, hb, y, mlstm_norm.reshape(1, nh * dv))


def _out_proj_kernel(a_ref, m_ref, w1_ref, w2_ref, h_ref, mod_ref, o_ref, w1_sc, w2_sc, *, n_ctx, tm, gate_idx):
    @pl.when(pl.program_id(1) == 0)
    def _():
        w1_sc[...] = w1_ref[...].astype(BF16)
        w2_sc[...] = w2_ref[...].astype(BF16)

    acc = jnp.dot(a_ref[...], w1_sc[...], preferred_element_type=F32)
    acc = acc + jnp.dot(m_ref[...], w2_sc[...], preferred_element_type=F32)
    rows = pl.program_id(1) * tm + lax.broadcasted_iota(jnp.int32, (tm, 1), 0)
    o_ref[...] = h_ref[...] + jnp.tanh(_stream_vec(mod_ref, gate_idx, rows, n_ctx)) * acc


def out_projection(a, m, w_out, h, mod, n_ctx, gate_idx):
    t, half = a.shape
    d = w_out.shape[1]
    tm = _pick_tile(t, (768, 512, 256))
    tn = _pick_tile(d, (512, 256, 128))
    return pl.pallas_call(
        functools.partial(_out_proj_kernel, n_ctx=n_ctx, tm=tm, gate_idx=gate_idx),
        grid=(d // tn, t // tm),
        in_specs=[
            pl.BlockSpec((tm, half), lambda j, i: (i, 0)),
            pl.BlockSpec((tm, half), lambda j, i: (i, 0)),
            pl.BlockSpec((half, tn), lambda j, i: (0, j)),
            pl.BlockSpec((half, tn), lambda j, i: (1, j)),
            pl.BlockSpec((tm, tn), lambda j, i: (i, j)),
            pl.BlockSpec((2, 6, tn), lambda j, i: (0, 0, j)),
        ],
        out_specs=pl.BlockSpec((tm, tn), lambda j, i: (i, j)),
        out_shape=jax.ShapeDtypeStruct((t, d), F32),
        scratch_shapes=[pltpu.VMEM((half, tn), BF16), pltpu.VMEM((half, tn), BF16)],
        compiler_params=_cparams(("arbitrary", "arbitrary"),
                                 2 * (2 * tm * half * 2 + 2 * half * tn * 4 + 2 * tm * tn * 4) + 2 * half * tn * 2
                                 + tm * tn * 4),
        name="out_proj",
    )(a, m, w_out, w_out, h, mod)


def _pool_kernel(h_ref, hp_ref, hn_ref, g_ref, mod_ref, o_ref, z_sc, *, seq, tm):
    i = pl.program_id(0)
    nblk = pl.num_programs(0)
    halo = POOL_HALO
    d = h_ref.shape[1]
    gd = d // len(POOL_WINDOWS)

    def normed(x):
        y = x * lax.rsqrt(jnp.mean(x * x, axis=-1, keepdims=True) + EPS) * g_ref[...]
        return y * (1.0 + mod_ref[0, 1:2, :]) + mod_ref[0, 0:1, :]

    z_sc[0:halo, :] = jnp.where(i > 0, normed(hp_ref[...]), 0.0)
    z_sc[halo:halo + tm, :] = normed(h_ref[...])
    z_sc[halo + tm:2 * halo + tm, :] = jnp.where(i < nblk - 1, normed(hn_ref[...]), 0.0)

    t = i * tm + lax.broadcasted_iota(jnp.int32, (tm, 1), 0)
    for gi, w in enumerate(POOL_WINDOWS):
        left = w // 2
        right = w - 1 - left
        cols = slice(gi * gd, (gi + 1) * gd)
        acc = z_sc[halo - left:halo - left + tm, cols]
        for off in range(-left + 1, right + 1):
            acc = acc + z_sc[halo + off:halo + off + tm, cols]
        cnt = (jnp.minimum(t + right + 1, seq) - jnp.maximum(t - left, 0)).astype(F32)
        o_ref[:, cols] = (acc / cnt - z_sc[halo:halo + tm, cols]).astype(BF16)


def pool_features(h, g, mod):
    seq, d = h.shape
    tm = ROW_TILE
    halo = POOL_HALO
    r = tm // halo
    nhb = seq // halo
    return pl.pallas_call(
        functools.partial(_pool_kernel, seq=seq, tm=tm),
        grid=(seq // tm,),
        in_specs=[
            pl.BlockSpec((tm, d), lambda i: (i, 0)),
            pl.BlockSpec((halo, d), lambda i: (jnp.maximum(i * r - 1, 0), 0)),
            pl.BlockSpec((halo, d), lambda i: (jnp.minimum((i + 1) * r, nhb - 1), 0)),
            pl.BlockSpec((1, d), lambda i: (0, 0)),
            pl.BlockSpec((2, 6, d), lambda i: (0, 0, 0)),
        ],
        out_specs=pl.BlockSpec((tm, d), lambda i: (i, 0)),
        out_shape=jax.ShapeDtypeStruct((seq, d), BF16),
        scratch_shapes=[pltpu.VMEM((tm + 2 * halo, d), F32)],
        compiler_params=_cparams(("arbitrary",), 10 * tm * d * 4),
        name="pool_features",
    )(h, h, h, g.reshape(1, d), mod)


def _pool_proj_kernel(x_ref, w_ref, ps_ref, h_ref, mod_ref, o_ref):
    acc = jnp.dot(x_ref[...], w_ref[0], preferred_element_type=F32)
    o_ref[...] = h_ref[...] + jnp.tanh(mod_ref[0, 2:3, :]) * (acc * ps_ref[...])


def pool_projection(dp, pool_w, pool_scale, h, mod):
    seq, d = h.shape
    ng, gd, _ = pool_w.shape
    tm = _pick_tile(seq, (1024, 512, 256))
    return pl.pallas_call(
        _pool_proj_kernel,
        grid=(ng, seq // tm),
        in_specs=[
            pl.BlockSpec((tm, gd), lambda g, i: (i, g)),
            pl.BlockSpec((1, gd, gd), lambda g, i: (g, 0, 0)),
            pl.BlockSpec((1, gd), lambda g, i: (0, g)),
            pl.BlockSpec((tm, gd), lambda g, i: (i, g)),
            pl.BlockSpec((2, 6, gd), lambda g, i: (0, 0, g)),
        ],
        out_specs=pl.BlockSpec((tm, gd), lambda g, i: (i, g)),
        out_shape=jax.ShapeDtypeStruct((seq, d), F32),
        compiler_params=_cparams(("arbitrary", "arbitrary"), 2 * (tm * gd * 10 + gd * gd * 2) + tm * gd * 4),
        name="pool_proj",
    )(dp, pool_w, pool_scale.reshape(1, d), h, mod)


def _route_kernel(lg_ref, bias_ref, gate_ref, idx_ref, rank_ref, cnt_ref, run_sc, *, n_exp):
    ng = N_GROUPS
    ge = n_exp // ng
    tt = lg_ref.shape[1]
    neg = -jnp.inf

    @pl.when(pl.program_id(0) == 0)
    def _():
        run_sc[...] = jnp.zeros_like(run_sc)

    scores = jax.nn.sigmoid(lg_ref[...])
    sel = (scores + bias_ref[...]).reshape(ng, ge, tt)
    e_in_g = lax.broadcasted_iota(jnp.int32, (ng, ge, tt), 1)
    m1 = jnp.max(sel, axis=1, keepdims=True)
    first = jnp.min(jnp.where(sel == m1, e_in_g, ge), axis=1, keepdims=True)
    m2 = jnp.max(jnp.where(e_in_g == first, neg, sel), axis=1, keepdims=True)
    gscore = m1 + m2

    gid = lax.broadcasted_iota(jnp.int32, (ng, 1, tt), 0)
    gmask = jnp.zeros((ng, 1, tt), jnp.bool_)
    for _ in range(TOPK_GROUPS):
        mx = jnp.max(gscore, axis=0, keepdims=True)
        pick = gid == jnp.min(jnp.where(gscore == mx, gid, ng), axis=0, keepdims=True)
        gmask = jnp.logical_or(gmask, pick)
        gscore = jnp.where(pick, neg, gscore)

    eid = lax.broadcasted_iota(jnp.int32, (ng, ge, tt), 0) * ge + e_in_g
    cand = jnp.where(gmask, sel, neg)
    chosen = jnp.zeros((ng, ge, tt), jnp.bool_)
    for kk in range(TOP_K):
        mx = jnp.max(jnp.max(cand, axis=1, keepdims=True), axis=0, keepdims=True)
        hit = jnp.where(cand == mx, eid, n_exp)
        pick_id = jnp.min(jnp.min(hit, axis=1, keepdims=True), axis=0, keepdims=True)
        pick = eid == pick_id
        chosen = jnp.logical_or(chosen, pick)
        cand = jnp.where(pick, neg, cand)
        idx_ref[kk:kk + 1, :] = pick_id.reshape(1, tt)

    s3 = scores.reshape(ng, ge, tt)
    w = jnp.where(chosen, s3, 0.0)
    denom = jnp.sum(jnp.sum(w, axis=1, keepdims=True), axis=0, keepdims=True)
    gate_ref[...] = (w / denom * ROUTED_SCALE).reshape(n_exp, tt)

    chosen_f = chosen.reshape(n_exp, tt).astype(F32)
    earlier = (lax.broadcasted_iota(jnp.int32, (tt, tt), 0) < lax.broadcasted_iota(jnp.int32, (tt, tt), 1))
    local = jnp.dot(chosen_f.astype(BF16), earlier.astype(BF16), preferred_element_type=F32)
    run = run_sc[...]
    rank_ref[...] = jnp.where(chosen.reshape(n_exp, tt), local + run, -1.0).astype(jnp.int32)
    run = run + jnp.sum(chosen_f, axis=1, keepdims=True)
    run_sc[...] = run
    cnt_ref[...] = jnp.broadcast_to(run, cnt_ref.shape).astype(jnp.int32)


def route(logits_t, router_bias):
    n_exp, t = logits_t.shape
    tt = _pick_tile(t, (512, 256, 128))
    gates, idx, rank, cnt = pl.pallas_call(
        functools.partial(_route_kernel, n_exp=n_exp),
        grid=(t // tt,),
        in_specs=[pl.BlockSpec((n_exp, tt), lambda i: (0, i)), pl.BlockSpec((n_exp, 1), lambda i: (0, 0))],
        out_specs=[
            pl.BlockSpec((n_exp, tt), lambda i: (0, i)),
            pl.BlockSpec((TOP_K, tt), lambda i: (0, i)),
            pl.BlockSpec((n_exp, tt), lambda i: (0, i)),
            pl.BlockSpec((n_exp, LANES), lambda i: (0, 0)),
        ],
        out_shape=[
            jax.ShapeDtypeStruct((n_exp, t), F32),
            jax.ShapeDtypeStruct((TOP_K, t), jnp.int32),
            jax.ShapeDtypeStruct((n_exp, t), jnp.int32),
            jax.ShapeDtypeStruct((n_exp, LANES), jnp.int32),
        ],
        scratch_shapes=[pltpu.VMEM((n_exp, 1), F32)],
        compiler_params=_cparams(("arbitrary",), 32 * 2**20),
        name="moe_route",
    )(logits_t, router_bias.reshape(n_exp, 1))
    return gates, idx, rank, cnt[:, 0]


def _slab_copy(src, src_row, dst, dst_row, sem):
    return pltpu.make_async_copy(src.at[src_row], dst.at[dst_row], sem)


def _dispatch_kernel(pos_ref, pad_ref, zp_ref, xs_ref, zero_sc, sem, pad_sem, *, tt, n_exp, tile, n_tiles):
    i = pl.program_id(0)

    def for_each_zero_copy(fn):
        def per_expert(e, carry):
            first = pad_ref[0, e]

            def one(r, c):
                fn(pltpu.make_async_copy(zero_sc.at[0], xs_ref.at[first + r], pad_sem))
                return c

            lax.fori_loop(0, pad_ref[1, e], one, 0)
            return carry

        lax.fori_loop(0, n_exp, per_expert, 0)

        def tail(tile_idx, c):
            fn(pltpu.make_async_copy(zero_sc, xs_ref.at[pl.ds(tile_idx * tile, tile)], pad_sem))
            return c

        lax.fori_loop(pad_ref[2, 0], n_tiles, tail, 0)

    @pl.when(i == 0)
    def _():
        zero_sc[...] = jnp.zeros_like(zero_sc)
        for_each_zero_copy(lambda cp: cp.start())

    def issue(j, carry):
        for k in range(TOP_K):
            _slab_copy(zp_ref, j, xs_ref, pos_ref[0, 0, j * TOP_K + k], sem).start()
        return carry

    lax.fori_loop(0, tt, issue, 0)
    for _ in range(tt * TOP_K):
        _slab_copy(zp_ref, 0, xs_ref, 0, sem).wait()

    @pl.when(i == pl.num_programs(0) - 1)
    def _():
        for_each_zero_copy(lambda cp: cp.wait())


def dispatch(zp, pos_tk, pad_info, n_tiles):
    t, ns, _ = zp.shape
    tt = COMBINE_TILE
    tile = EXPERT_TILE
    n_exp = pad_info.shape[1]
    return pl.pallas_call(
        functools.partial(_dispatch_kernel, tt=tt, n_exp=n_exp, tile=tile, n_tiles=n_tiles),
        grid=(t // tt,),
        in_specs=[
            pl.BlockSpec((1, 1, tt * TOP_K), lambda i: (i, 0, 0), memory_space=pltpu.SMEM),
            pl.BlockSpec(memory_space=pltpu.SMEM),
            pl.BlockSpec((tt, ns, LANES), lambda i: (i, 0, 0)),
        ],
        out_specs=pl.BlockSpec(memory_space=pl.ANY),
        out_shape=jax.ShapeDtypeStruct((n_tiles * tile, ns, LANES), jnp.uint32),
        scratch_shapes=[pltpu.VMEM((tile, ns, LANES), jnp.uint32), pltpu.SemaphoreType.DMA(()),
                        pltpu.SemaphoreType.DMA(())],
        compiler_params=_cparams(("arbitrary",), (2 * tt + tile) * ns * LANES * 4),
        name="moe_dispatch",
    )(pos_tk.reshape(t // tt, 1, tt * TOP_K), pad_info, zp)


def _expert_kernel(te_ref, first_ref, nused_ref, x_ref, wg_ref, wu_ref, wd_ref, y_ref, wg_sc, wu_sc, wd_sc):
    i = pl.program_id(0)

    @pl.when(i < nused_ref[0])
    def _():
        @pl.when(first_ref[i] == 1)
        def _():
            wg_sc[...] = wg_ref[0, 0].astype(BF16)
            wu_sc[...] = wu_ref[0, 0].astype(BF16)
            wd_sc[...] = wd_ref[0, 0].astype(BF16)

        x = _unpack_rows(_load_slabs(x_ref)).astype(BF16)
        g = jnp.dot(x, wg_sc[...], preferred_element_type=F32)
        u = jnp.dot(x, wu_sc[...], preferred_element_type=F32)
        hid = (g * jax.nn.sigmoid(g) * u).astype(BF16)
        y = jnp.dot(hid, wd_sc[...], preferred_element_type=F32)
        _store_slabs(y_ref, _pack_rows(y))

    @pl.when(i >= nused_ref[0])
    def _():
        y_ref[...] = jnp.zeros_like(y_ref)


def expert_ffn(xs, tile_expert, tile_first, n_used, layer, exp_gate, exp_up, exp_down):
    n_rows, ns, _ = xs.shape
    _, n_exp, d, f = exp_gate.shape
    tm = EXPERT_TILE
    n_tiles = n_rows // tm

    def row_blk(i, te, first, nused):
        return (jnp.minimum(i, nused[0] - 1), 0, 0)

    def w_blk(i, te, first, nused):
        return (layer, te[i], 0, 0)

    grid_spec = pltpu.PrefetchScalarGridSpec(
        num_scalar_prefetch=3,
        grid=(n_tiles,),
        in_specs=[
            pl.BlockSpec((tm, ns, LANES), row_blk),
            pl.BlockSpec((1, 1, d, f), w_blk),
            pl.BlockSpec((1, 1, d, f), w_blk),
            pl.BlockSpec((1, 1, f, d), w_blk),
        ],
        out_specs=pl.BlockSpec((tm, ns, LANES), lambda i, te, first, nused: (i, 0, 0)),
        scratch_shapes=[pltpu.VMEM((d, f), BF16), pltpu.VMEM((d, f), BF16), pltpu.VMEM((f, d), BF16)],
    )
    return pl.pallas_call(
        _expert_kernel,
        grid_spec=grid_spec,
        out_shape=jax.ShapeDtypeStruct((n_rows, ns, LANES), jnp.uint32),
        compiler_params=_cparams(("arbitrary",), 2 * 3 * d * f * 4 + 3 * d * f * 2 + 4 * tm * ns * LANES * 4
                                 + 6 * tm * d * 4),
        name="moe_experts",
    )(tile_expert, tile_first, n_used, xs, exp_gate, exp_up, exp_down)


def _combine_kernel(pos_ref, pos_next_ref, ys_ref, wt_ref, z_ref, sg_ref, su_ref, sd_ref, h_ref, mod_ref, o_ref,
                    buf, sem, *, tt, n_ctx, row0):
    i = pl.program_id(0)
    slot = i % 2

    def issue(p_ref, s):
        def body(j, carry):
            for k in range(TOP_K):
                _slab_copy(ys_ref, p_ref[0, 0, j * TOP_K + k], buf.at[s, k], j, sem.at[s]).start()
            return carry

        lax.fori_loop(0, tt, body, 0)

    @pl.when(i == 0)
    def _():
        issue(pos_ref, 0)

    @pl.when(i + 1 < pl.num_programs(0))
    def _():
        issue(pos_next_ref, 1 - slot)

    z = z_ref[...]
    g = jnp.dot(z, sg_ref[...], preferred_element_type=F32)
    u = jnp.dot(z, su_ref[...], preferred_element_type=F32)
    acc = jnp.dot((g * jax.nn.sigmoid(g) * u).astype(BF16), sd_ref[...], preferred_element_type=F32)

    for _ in range(tt * TOP_K):
        _slab_copy(ys_ref, 0, buf.at[slot, 0], 0, sem.at[slot]).wait()

    wt = wt_ref[...]
    for k in range(TOP_K):
        acc = acc + wt[:, k:k + 1] * _unpack_rows(_load_slabs(buf.at[slot, k]))
    rows = row0 + pl.program_id(0) * tt + lax.broadcasted_iota(jnp.int32, (tt, 1), 0)
    o_ref[...] = h_ref[...] + jnp.tanh(_stream_vec(mod_ref, 5, rows, n_ctx)) * acc


def combine(ys, pos_tk, w_tk, z, sh_gate, sh_up, sh_down, h, mod, n_ctx, row0):
    t, d = h.shape
    tt = COMBINE_TILE
    f = sh_gate.shape[1]
    ns = ys.shape[1]
    b0 = row0 // tt
    n_out = t - row0
    last = t // tt - 1
    pos_blocks = pos_tk.reshape(t // tt, 1, tt * TOP_K)
    return pl.pallas_call(
        functools.partial(_combine_kernel, tt=tt, n_ctx=n_ctx, row0=row0),
        grid=(n_out // tt,),
        in_specs=[
            pl.BlockSpec((1, 1, tt * TOP_K), lambda i: (b0 + i, 0, 0), memory_space=pltpu.SMEM),
            pl.BlockSpec((1, 1, tt * TOP_K), lambda i: (jnp.minimum(b0 + i + 1, last), 0, 0), memory_space=pltpu.SMEM),
            pl.BlockSpec(memory_space=pl.ANY),
            pl.BlockSpec((tt, TOP_K), lambda i: (b0 + i, 0)),
            pl.BlockSpec((tt, d), lambda i: (b0 + i, 0)),
            pl.BlockSpec((d, f), lambda i: (0, 0)),
            pl.BlockSpec((d, f), lambda i: (0, 0)),
            pl.BlockSpec((f, d), lambda i: (0, 0)),
            pl.BlockSpec((tt, d), lambda i: (b0 + i, 0)),
            pl.BlockSpec((2, 6, d), lambda i: (0, 0, 0)),
        ],
        out_specs=pl.BlockSpec((tt, d), lambda i: (i, 0)),
        out_shape=jax.ShapeDtypeStruct((n_out, d), F32),
        scratch_shapes=[pltpu.VMEM((2, TOP_K, tt, ns, LANES), jnp.uint32), pltpu.SemaphoreType.DMA((2,))],
        compiler_params=_cparams(("arbitrary",),
                                 2 * TOP_K * tt * ns * LANES * 4 + 12 * tt * d * 4 + 6 * d * f * 2 * 2),
        name="moe_combine",
    )(pos_blocks, pos_blocks, ys, w_tk, z, sh_gate, sh_up, sh_down, h, mod)


def moe_block(h, norm_g, mod, n_ctx, row0, layer, router_w, router_bias, exp_gate, exp_up, exp_down,
              sh_gate, sh_up, sh_down):
    t, d = h.shape
    n_exp = router_w.shape[1]
    z, zp, logits = norm_modulate(h, norm_g, mod, n_ctx, 3, router_w=router_w)
    gates, idx_t, rank, counts = route(logits[:, :n_exp].T, router_bias)

    tm = EXPERT_TILE
    padded = (counts + tm - 1) // tm * tm
    ends = jnp.cumsum(padded)
    starts = ends - padded
    n_tiles = (t * TOP_K) // tm + n_exp
    tile_start = jnp.arange(n_tiles, dtype=jnp.int32) * tm
    n_used = (ends[-1] // tm).astype(jnp.int32)
    tile_expert = jnp.sum((ends[None, :] <= tile_start[:, None]).astype(jnp.int32), axis=1)
    tile_expert = jnp.minimum(tile_expert, n_exp - 1)
    tile_expert = jnp.where(tile_start < ends[-1], tile_expert, tile_expert[jnp.maximum(n_used - 1, 0)])
    tile_first = jnp.concatenate([jnp.ones((1,), jnp.int32), (tile_expert[1:] != tile_expert[:-1]).astype(jnp.int32)])

    pos_dense = starts[:, None] + rank
    pos_tk = jnp.take_along_axis(pos_dense, idx_t, axis=0).T.astype(jnp.int32)
    w_tk = jnp.take_along_axis(gates, idx_t, axis=0).T

    pad_info = jnp.stack([starts + counts, padded - counts, jnp.full_like(counts, n_used)]).astype(jnp.int32)
    xs = dispatch(zp, pos_tk, pad_info, n_tiles)
    ys = expert_ffn(xs, tile_expert, tile_first, n_used.reshape(1), layer, exp_gate, exp_up, exp_down)
    return combine(ys, pos_tk, w_tk, z, sh_gate.astype(BF16), sh_up.astype(BF16), sh_down.astype(BF16),
                   h, mod, n_ctx, row0)


def mixer_layer(h, n_ctx, mod, norm_g, w_in, gate_bias, q_norm, k_norm, mlstm_norm, w_out, cos_t, sin_t):
    t, d = h.shape
    dh = ATTN_HEAD_DIM
    n_q = (d // 2) // dh
    n_kv = n_q // GQA_GROUP
    nh = MLSTM_HEADS
    dv = (d // 2) // nh
    dk = dv // 2
    sizes = (n_q * dh, n_kv * dh, n_kv * dh, nh * dk, nh * dk, nh * dv, nh * dv, 4 * nh)
    off = [0]
    for s in sizes:
        off.append(off[-1] + s)
    n_main = off[7]

    z = norm_modulate(h, norm_g, mod, n_ctx, 0)
    y = matmul(z, w_in, n_main, BF16)
    wg = jnp.zeros((d, LANES), F32).at[:, :4 * nh].set(w_in[:, n_main:])
    gates_raw = matmul(z, wg, LANES, F32)

    qt, k, vt = qk_prepare(y, cos_t, sin_t, q_norm, k_norm, n_q, n_kv)
    a = attention(qt, k, vt, n_ctx)
    hf, hb = mlstm(y, gates_raw, gate_bias, off[3], off[4], off[5], dk, dv)
    m = mlstm_output(hf, hb, y, mlstm_norm, off[6], dv)
    return out_projection(a, m, w_out, h, mod, n_ctx, 2)


def kernel(x, c, ctx, c_ctx, ada_w, ada_b, norm_mix, norm_ffn, w_in, gate_bias, q_norm, k_norm, mlstm_norm, w_out,
           pool_w, pool_scale, router_w, router_bias, exp_gate, exp_up, exp_down, sh_gate, sh_up, sh_down):
    depth = ada_w.shape[0]
    seq = x.shape[1]
    n_ctx = ctx.shape[1]
    assert x.shape[0] == 1 and n_ctx == ROW_TILE and seq % ROW_TILE == 0
    mods = ada_modulation(c, c_ctx, ada_w, ada_b)
    cos_t, sin_t = rope_tables(n_ctx, seq)

    h = jnp.concatenate([ctx[0], x[0]], axis=0)
    nc = n_ctx
    for layer in range(depth):
        last = layer == depth - 1
        j = layer // 2
        mod = mods[layer]
        if layer % 2 == 0:
            h = mixer_layer(h, nc, mod, norm_mix[layer], w_in[j], gate_bias[j], q_norm[j], k_norm[j],
                            mlstm_norm[j], w_out[j], cos_t, sin_t)
        else:
            assert nc == 0 or not last
            dp = pool_features(h, norm_mix[layer], mod)
            h = pool_projection(dp, pool_w[j].astype(BF16), pool_scale[j], h, mod)
        drop_ctx = nc > 0 and (last or (layer + 1 == depth - 1 and (depth - 1) % 2 == 1))
        row0 = nc if drop_ctx else 0
        h = moe_block(h, norm_ffn[layer], mod, nc, row0, layer, router_w[layer], router_bias[layer], exp_gate,
                      exp_up, exp_down, sh_gate[layer], sh_up[layer], sh_down[layer])
        if drop_ctx:
            nc = 0
    return h[nc:][None]
```
